```python
import jax, jax.numpy as jnp
from jax import lax
import numpy as np

D_MODEL = 2048
BATCH = 2
SEQ = 8192
DEPTH = 4

N_RET_LAYERS = DEPTH // 2
N_ATT_LAYERS = DEPTH - N_RET_LAYERS

RET_HEADS = 8
RET_QK_DIM = D_MODEL // RET_HEADS
RET_V_DIM = 2 * RET_QK_DIM
RET_CHUNK = 128
ROPE_BASE = 10000.0

DIL_CONFIGS = ((128, 1), (512, 4), (2048, 16))
N_GROUPS = len(DIL_CONFIGS)
ATT_HEAD_DIM = 128
ATT_HEADS = D_MODEL // ATT_HEAD_DIM
REL_BUCKETS = 32
REL_MAX_DIST = 2048
FFN_DIM = -(-8 * D_MODEL // (3 * 256)) * 256
NORM_EPS = 1e-6
NEG_INF = -1e30

kernel_name = "yoco_retention_dilated_attention_trunk"


def _rms(x, g):
    xf = x.astype(jnp.float32)
    y = xf * lax.rsqrt(jnp.mean(xf * xf, axis=-1, keepdims=True) + NORM_EPS)
    return (y * g.astype(jnp.float32)).astype(x.dtype)


def _swiglu(h, w_in, w_out):
    z = h @ w_in
    return (jax.nn.silu(z[..., :FFN_DIM]) * z[..., FFN_DIM:]) @ w_out


def _rope(t, cos, sin):
    half = t.shape[-1] // 2
    t1, t2 = t[..., :half], t[..., half:]
    return jnp.concatenate([t1 * cos - t2 * sin, t2 * cos + t1 * sin], axis=-1)


def _retention(q, k, v):
    B_, S_, H_, dk = q.shape
    dv = v.shape[-1]
    C = RET_CHUNK
    N = S_ // C
    log_g = np.log(1.0 - 2.0 ** (-5.0 - np.arange(H_))).astype(np.float32)
    idx = np.arange(C)
    diff = idx[:, None] - idx[None, :]
    dmask = np.where(diff[None] >= 0, np.exp(log_g[:, None, None] * np.maximum(diff, 0)[None]), 0.0)
    q_dec = np.exp(log_g[None, :] * (idx[:, None] + 1))
    k_dec = np.exp(log_g[None, :] * (C - 1 - idx)[:, None])
    c_dec = np.exp(log_g * C)
    dmask = jnp.asarray(dmask, dtype=q.dtype)
    q_dec = jnp.asarray(q_dec, dtype=q.dtype)
    k_dec = jnp.asarray(k_dec, dtype=q.dtype)
    c_dec = jnp.asarray(c_dec, dtype=q.dtype)

    def chunks(t):
        return jnp.moveaxis(t.reshape(B_, N, C, H_, t.shape[-1]), 1, 0)

    def step(R, inp):
        qc, kc, vc = inp
        s = jnp.einsum('bihd,bjhd->bhij', qc, kc) * dmask
        inner = jnp.einsum('bhij,bjhe->bihe', s, vc)
        cross = jnp.einsum('bihd,bhde->bihe', qc, R) * q_dec[None, :, :, None]
        R = R * c_dec[None, :, None, None] + jnp.einsum('bjhd,bjhe->bhde', kc * k_dec[None, :, :, None], vc)
        return R, inner + cross

    R0 = jnp.zeros((B_, H_, dk, dv), q.dtype)
    _, y = lax.scan(step, R0, (chunks(q), chunks(k), chunks(v)))
    return jnp.moveaxis(y, 0, 1).reshape(B_, S_, H_, dv)


def _retention_layer(h, w_in, w_out):
    B_, S_, _ = h.shape
    nq = RET_HEADS * RET_QK_DIM
    nv = RET_HEADS * RET_V_DIM
    z = h @ w_in
    q = z[..., :nq].reshape(B_, S_, RET_HEADS, RET_QK_DIM)
    k = z[..., nq:2 * nq].reshape(B_, S_, RET_HEADS, RET_QK_DIM) * (RET_QK_DIM ** -0.5)
    v = z[..., 2 * nq:2 * nq + nv].reshape(B_, S_, RET_HEADS, RET_V_DIM)
    g = z[..., 2 * nq + nv:]
    inv = (1.0 / ROPE_BASE ** np.linspace(0.0, 1.0, RET_QK_DIM // 2)).astype(np.float32)
    ang = jnp.arange(S_, dtype=jnp.float32)[:, None] * jnp.asarray(inv)[None, :]
    cos = jnp.cos(ang)[:, None, :].astype(h.dtype)
    sin = jnp.sin(ang)[:, None, :].astype(h.dtype)
    y = _retention(_rope(q, cos, sin), _rope(k, cos, sin), v)
    yf = y.astype(jnp.float32)
    y = (yf * lax.rsqrt(jnp.mean(yf * yf, axis=-1, keepdims=True) + NORM_EPS)).astype(h.dtype)
    y = y.reshape(B_, S_, nv) * jax.nn.silu(g)
    return y @ w_out


def _t5_bucket(dist):
    n = np.maximum(dist, 0)
    max_exact = REL_BUCKETS // 2
    large = max_exact + (np.log(np.maximum(n, 1) / max_exact) / np.log(REL_MAX_DIST / max_exact)
                         * (REL_BUCKETS - max_exact)).astype(np.int32)
    large = np.minimum(large, REL_BUCKETS - 1)
    return np.where(n < max_exact, n, large).astype(np.int32)


def _to_strided(t, d, blk):
    B_, S_, H_, E_ = t.shape
    L = S_ // d
    nb = -(-L // blk)
    t = t.reshape(B_, L, d, H_, E_).transpose(0, 2, 1, 3, 4)
    t = jnp.pad(t, ((0, 0), (0, 0), (0, nb * blk - L), (0, 0), (0, 0)))
    return t.reshape(B_, d, nb, blk, H_, E_)


def _from_strided(t, S_):
    B_, d, nb, blk = t.shape[:4]
    rest = t.shape[4:]
    t = t.reshape((B_, d, nb * blk) + rest)[:, :, :S_ // d]
    t = jnp.moveaxis(t, 1, 2)
    return t.reshape((B_, S_) + rest)


def _band(t):
    prev = jnp.pad(t[:, :, :-1], ((0, 0), (0, 0), (1, 0), (0, 0), (0, 0), (0, 0)))
    return jnp.concatenate([prev, t], axis=3)


def _shared_kv(x, g_kv, w_kv, rel_bias):
    B_, S_, _ = x.shape
    kv = (_rms(x, g_kv) @ w_kv).reshape(B_, S_, 2, N_GROUPS, ATT_HEADS, ATT_HEAD_DIM)
    shared, patterns = [], []
    for gi, (window, d) in enumerate(DIL_CONFIGS):
        blk = window // d
        nb = -(-(S_ // d) // blk)
        kb = _to_strided(kv[:, :, 0, gi], d, blk)
        vb = _to_strided(kv[:, :, 1, gi], d, blk)
        i = np.arange(blk)[:, None]
        c = np.arange(2 * blk)[None, :]
        delta = blk + i - c
        band = (delta >= 0) & (delta <= blk)
        bucket = _t5_bucket(np.maximum(delta, 0) * d)
        table_g = rel_bias[:, gi * ATT_HEADS:(gi + 1) * ATT_HEADS].astype(jnp.float32)
        bias = jnp.moveaxis(table_g[bucket], -1, 0)
        mask = band[None] & ((np.arange(nb)[:, None, None] > 0) | (c >= blk)[None])
        shared.append((kb, vb))
        patterns.append((bias, jnp.asarray(mask)))
    return shared, patterns


def _dilated_group(q, kb, vb, bias, mask, d):
    S_ = q.shape[1]
    blk = kb.shape[3]
    qb = _to_strided(q, d, blk)
    kband = _band(kb)
    vband = _band(vb)
    s = jnp.einsum('brnihe,brnjhe->brnhij', qb, kband).astype(jnp.float32) * (ATT_HEAD_DIM ** -0.5) + bias
    s = jnp.where(mask[None, None, :, None], s, NEG_INF)
    m = jnp.max(s, axis=-1, keepdims=True)
    p = jnp.exp(s - m)
    den = jnp.sum(p, axis=-1, keepdims=True)
    o = jnp.einsum('brnhij,brnjhe->brnihe', (p / den).astype(vb.dtype), vband)
    lse = jnp.moveaxis((m + jnp.log(den))[..., 0], 3, 4)
    return _from_strided(o, S_), _from_strided(lse, S_)


def _dilated_layer(h, w_q, w_out, shared, patterns):
    B_, S_, _ = h.shape
    q = (h @ w_q).reshape(B_, S_, N_GROUPS, ATT_HEADS, ATT_HEAD_DIM)
    outs, lses = [], []
    for gi, (window, d) in enumerate(DIL_CONFIGS):
        kb, vb = shared[gi]
        bias, mask = patterns[gi]
        o, lse = _dilated_group(q[:, :, gi], kb, vb, bias, mask, d)
        outs.append(o)
        lses.append(lse)
    wts = jax.nn.softmax(jnp.stack(lses, axis=0), axis=0)
    o = jnp.einsum('gbsh,gbshe->bshe', wts.astype(h.dtype), jnp.stack(outs, axis=0))
    return o.reshape(B_, S_, ATT_HEADS * ATT_HEAD_DIM) @ w_out


def setup_inputs(seed: int = 0) -> dict:
    key = jax.random.key(seed)
    ks = jax.random.split(key, 13)
    res = (2.0 * DEPTH) ** -0.5
    nq = RET_HEADS * RET_QK_DIM
    nv = RET_HEADS * RET_V_DIM
    natt = N_GROUPS * ATT_HEADS * ATT_HEAD_DIM

    def nrm(k, shape, fan_in, scale=1.0):
        return jax.random.normal(k, shape, jnp.float32) * (scale * fan_in ** -0.5)

    def gain(k, shape):
        return 1.0 + 0.05 * jax.random.normal(k, shape, jnp.float32)

    return {
        "x": jax.random.normal(ks[0], (BATCH, SEQ, D_MODEL), jnp.float32),
        "g_mix": gain(ks[1], (DEPTH, D_MODEL)),
        "g_ffn": gain(ks[2], (DEPTH, D_MODEL)),
        "w_ret_in": nrm(ks[3], (N_RET_LAYERS, D_MODEL, 2 * nq + 2 * nv), D_MODEL),
        "w_ret_out": nrm(ks[4], (N_RET_LAYERS, nv, D_MODEL), nv, res),
        "g_kv": gain(ks[5], (D_MODEL,)),
        "w_kv": nrm(ks[6], (D_MODEL, 2 * natt), D_MODEL),
        "w_att_q": nrm(ks[7], (N_ATT_LAYERS, D_MODEL, natt), D_MODEL),
        "w_att_out": nrm(ks[8], (N_ATT_LAYERS, ATT_HEADS * ATT_HEAD_DIM, D_MODEL), ATT_HEADS * ATT_HEAD_DIM, res),
        "rel_bias": 0.1 * jax.random.normal(ks[9], (REL_BUCKETS, N_GROUPS * ATT_HEADS), jnp.float32),
        "w_ffn_in": nrm(ks[10], (DEPTH, D_MODEL, 2 * FFN_DIM), D_MODEL),
        "w_ffn_out": nrm(ks[11], (DEPTH, FFN_DIM, D_MODEL), FFN_DIM, res),
        "g_final": gain(ks[12], (D_MODEL,)),
    }


def reference(x, g_mix, g_ffn, w_ret_in, w_ret_out, g_kv, w_kv, w_att_q, w_att_out, rel_bias, w_ffn_in, w_ffn_out, g_final):
    shared, patterns = None, None
    for l in range(DEPTH):
        if l < N_RET_LAYERS:
            x = x + _retention_layer(_rms(x, g_mix[l]), w_ret_in[l], w_ret_out[l])
        else:
            if l == N_RET_LAYERS:
                shared, patterns = _shared_kv(x, g_kv, w_kv, rel_bias)
            j = l - N_RET_LAYERS
            x = x + _dilated_layer(_rms(x, g_mix[l]), w_att_q[j], w_att_out[j], shared, patterns)
        x = x + _swiglu(_rms(x, g_ffn[l]), w_ffn_in[l], w_ffn_out[l])
    return _rms(x, g_final)
```

```python
import functools

import numpy as np
import jax
import jax.numpy as jnp
from jax import lax
from jax.experimental import pallas as pl
from jax.experimental.pallas import tpu as pltpu

F32 = jnp.float32
BF16 = jnp.bfloat16

D_MODEL = 2048
DEPTH = 4
N_RET_LAYERS = DEPTH // 2

RET_HEADS = 8
RET_QK_DIM = D_MODEL // RET_HEADS
RET_V_DIM = 2 * RET_QK_DIM
RET_NQ = RET_HEADS * RET_QK_DIM
RET_NV = RET_HEADS * RET_V_DIM
ROPE_BASE = 10000.0
ROPE_HALF = RET_QK_DIM // 2

DIL_CONFIGS = ((128, 1), (512, 4), (2048, 16))
N_GROUPS = len(DIL_CONFIGS)
ATT_HEAD_DIM = 128
ATT_HEADS = D_MODEL // ATT_HEAD_DIM
ATT_WIDTH = ATT_HEADS * ATT_HEAD_DIM
ATT_BLK = 128
REL_BUCKETS = 32
REL_MAX_DIST = 2048
FFN_DIM = -(-8 * D_MODEL // (3 * 256)) * 256
NORM_EPS = 1e-6
NEG_INF = -1e30

LANES = 128
RET_CHUNK = 256
RMS_ROWS = 64
VMEM_LIMIT = 56 * 1024 * 1024


def _params(semantics):
    return pltpu.CompilerParams(dimension_semantics=semantics, vmem_limit_bytes=VMEM_LIMIT)


def _rms_rows(x_ref, g_ref, dst_ref, copy_ref=None):
    g = g_ref[...]

    def body(c, carry):
        r0 = pl.multiple_of(c * RMS_ROWS, RMS_ROWS)
        x = x_ref[pl.ds(r0, RMS_ROWS), :]
        ms = jnp.mean(x * x, axis=-1, keepdims=True)
        dst_ref[pl.ds(r0, RMS_ROWS), :] = (x * lax.rsqrt(ms + NORM_EPS) * g).astype(dst_ref.dtype)
        if copy_ref is not None:
            copy_ref[pl.ds(r0, RMS_ROWS), :] = x
        return carry

    lax.fori_loop(0, x_ref.shape[0] // RMS_ROWS, body, 0)


def _norm_matmul_kernel(*refs, rope_tiles, k_tile0, k_scale):
    if rope_tiles:
        x_ref, g_ref, w_ref, cos_ref, sin_ref, o_ref, h_ref = refs
    else:
        x_ref, g_ref, w_ref, o_ref, h_ref = refs
    j = pl.program_id(1)

    @pl.when(j == 0)
    def _():
        _rms_rows(x_ref, g_ref, h_ref)

    acc = jnp.dot(h_ref[...], w_ref[...], preferred_element_type=F32)

    if not rope_tiles:
        o_ref[...] = acc.astype(o_ref.dtype)
        return

    @pl.when(j >= rope_tiles)
    def _():
        o_ref[...] = acc.astype(o_ref.dtype)

    @pl.when(j < rope_tiles)
    def _():
        cos = cos_ref[...]
        sin = sin_ref[...]
        scale = jnp.where(j >= k_tile0, F32(k_scale), F32(1.0))
        for c in range(0, acc.shape[1], 2 * ROPE_HALF):
            t1 = acc[:, c:c + ROPE_HALF] * scale
            t2 = acc[:, c + ROPE_HALF:c + 2 * ROPE_HALF] * scale
            o_ref[:, c:c + ROPE_HALF] = (t1 * cos - t2 * sin).astype(o_ref.dtype)
            o_ref[:, c + ROPE_HALF:c + 2 * ROPE_HALF] = (t2 * cos + t1 * sin).astype(o_ref.dtype)


def _norm_matmul(x, gain, w, *, tm, tn, rope=None):
    t, d = x.shape
    n = w.shape[1]
    in_specs = [
        pl.BlockSpec((tm, d), lambda i, j: (i, 0)),
        pl.BlockSpec((1, d), lambda i, j: (0, 0)),
        pl.BlockSpec((d, tn), lambda i, j: (0, j)),
    ]
    args = [x, gain.reshape(1, d), w]
    if rope is None:
        kern = functools.partial(_norm_matmul_kernel, rope_tiles=0, k_tile0=0, k_scale=1.0)
    else:
        cos, sin, rope_cols, k_col0, k_scale, seq = rope
        pos_blocks = seq // tm
        in_specs += [pl.BlockSpec((tm, ROPE_HALF), lambda i, j: (i % pos_blocks, 0))] * 2
        args += [cos, sin]
        kern = functools.partial(_norm_matmul_kernel, rope_tiles=rope_cols // tn,
                                 k_tile0=k_col0 // tn, k_scale=k_scale)
    return pl.pallas_call(
        kern,
        out_shape=jax.ShapeDtypeStruct((t, n), BF16),
        grid=(t // tm, n // tn),
        in_specs=in_specs,
        out_specs=pl.BlockSpec((tm, tn), lambda i, j: (i, j)),
        scratch_shapes=[pltpu.VMEM((tm, d), BF16)],
        compiler_params=_params(("parallel", "arbitrary")),
        name="norm_matmul",
    )(*args)


def _retention_kernel(lg_ref, cd_ref, q_ref, k_ref, v_ref, g_ref, y_ref, r_ref, dm_ref, qd_ref, kd_ref):
    h = pl.program_id(1)
    n = pl.program_id(2)
    c = q_ref.shape[0]

    @pl.when(n == 0)
    def _():
        lg = lg_ref[h]
        r_ref[...] = jnp.zeros_like(r_ref)
        i = lax.broadcasted_iota(jnp.int32, (c, c), 0)
        jj = lax.broadcasted_iota(jnp.int32, (c, c), 1)
        diff = i - jj
        dm_ref[...] = jnp.where(diff >= 0, jnp.exp(lg * jnp.maximum(diff, 0).astype(F32)), 0.0)
        row = lax.broadcasted_iota(jnp.int32, (c, LANES), 0).astype(F32)
        qd_ref[...] = jnp.exp(lg * (row + 1.0))
        kd_ref[...] = jnp.exp(lg * (F32(c - 1) - row))

    q = q_ref[...]
    k = k_ref[...]
    v = v_ref[...]
    s = lax.dot_general(q, k, (((1,), (1,)), ((), ())), preferred_element_type=F32) * dm_ref[...]
    inner = jnp.dot(s.astype(BF16), v, preferred_element_type=F32)
    r_old = r_ref[...]
    cross = jnp.dot(q, r_old.astype(BF16), preferred_element_type=F32)
    qd = qd_ref[...]
    kd = kd_ref[...]
    dk = k.shape[1]
    k_dec = jnp.concatenate(
        [(k[:, o:o + LANES].astype(F32) * kd).astype(BF16) for o in range(0, dk, LANES)], axis=1)
    r_ref[...] = r_old * cd_ref[h] + lax.dot_general(
        k_dec, v, (((0,), (0,)), ((), ())), preferred_element_type=F32)

    dv = v.shape[1]
    y = jnp.concatenate(
        [inner[:, o:o + LANES] + cross[:, o:o + LANES] * qd for o in range(0, dv, LANES)], axis=1)
    ms = jnp.mean(y * y, axis=-1, keepdims=True)
    g = g_ref[...].astype(F32)
    gate = g * (1.0 / (1.0 + jnp.exp(-g)))
    y_ref[...] = ((y * lax.rsqrt(ms + NORM_EPS)) * gate).astype(y_ref.dtype)


def _retention(z, batch, seq):
    t = z.shape[0]
    c = RET_CHUNK
    nc = seq // c
    log_g = np.log(1.0 - 2.0 ** (-5.0 - np.arange(RET_HEADS))).astype(np.float32)
    c_dec = np.exp(log_g * c).astype(np.float32)
    kq = RET_NQ // RET_QK_DIM
    kv = 2 * RET_NQ // RET_V_DIM
    kg = kv + RET_HEADS
    row = lambda b, h, n: b * nc + n
    smem = pl.BlockSpec(memory_space=pltpu.SMEM)
    return pl.pallas_call(
        _retention_kernel,
        out_shape=jax.ShapeDtypeStruct((t, RET_NV), BF16),
        grid=(batch, RET_HEADS, nc),
        in_specs=[
            smem, smem,
            pl.BlockSpec((c, RET_QK_DIM), lambda b, h, n: (row(b, h, n), h)),
            pl.BlockSpec((c, RET_QK_DIM), lambda b, h, n: (row(b, h, n), kq + h)),
            pl.BlockSpec((c, RET_V_DIM), lambda b, h, n: (row(b, h, n), kv + h)),
            pl.BlockSpec((c, RET_V_DIM), lambda b, h, n: (row(b, h, n), kg + h)),
        ],
        out_specs=pl.BlockSpec((c, RET_V_DIM), lambda b, h, n: (row(b, h, n), h)),
        scratch_shapes=[
            pltpu.VMEM((RET_QK_DIM, RET_V_DIM), F32),
            pltpu.VMEM((c, c), F32),
            pltpu.VMEM((c, LANES), F32),
            pltpu.VMEM((c, LANES), F32),
        ],
        compiler_params=_params(("parallel", "parallel", "arbitrary")),
        name="retention",
    )(jnp.asarray(log_g), jnp.asarray(c_dec), z, z, z, z)


def _matmul_residual_kernel(y_ref, w_ref, x_ref, o_ref):
    o_ref[...] = x_ref[...] + jnp.dot(y_ref[...], w_ref[...], preferred_element_type=F32)


def _matmul_residual(y, w, x, *, tm, tn):
    t, k = y.shape
    n = w.shape[1]
    return pl.pallas_call(
        _matmul_residual_kernel,
        out_shape=jax.ShapeDtypeStruct((t, n), F32),
        grid=(n // tn, t // tm),
        in_specs=[
            pl.BlockSpec((tm, k), lambda j, i: (i, 0)),
            pl.BlockSpec((k, tn), lambda j, i: (0, j)),
            pl.BlockSpec((tm, tn), lambda j, i: (i, j)),
        ],
        out_specs=pl.BlockSpec((tm, tn), lambda j, i: (i, j)),
        compiler_params=_params(("parallel", "parallel")),
        name="matmul_residual",
    )(y, w, x)


def _ffn_kernel(*refs, final_norm):
    if final_norm:
        x_ref, g_ref, w1_ref, w2_ref, wo_ref, gf_ref, o_ref, h_ref = refs
    else:
        x_ref, g_ref, w1_ref, w2_ref, wo_ref, o_ref, h_ref = refs
    f = pl.program_id(1)

    @pl.when(f == 0)
    def _():
        _rms_rows(x_ref, g_ref, h_ref, copy_ref=o_ref)

    h = h_ref[...]
    z1 = jnp.dot(h, w1_ref[...], preferred_element_type=F32)
    z2 = jnp.dot(h, w2_ref[...], preferred_element_type=F32)
    a = (z1 * (1.0 / (1.0 + jnp.exp(-z1)))) * z2
    o_ref[...] += jnp.dot(a.astype(BF16), wo_ref[...], preferred_element_type=F32)

    if final_norm:
        @pl.when(f == pl.num_programs(1) - 1)
        def _():
            gf = gf_ref[...]

            def body(c, carry):
                r0 = pl.multiple_of(c * RMS_ROWS, RMS_ROWS)
                x = o_ref[pl.ds(r0, RMS_ROWS), :]
                ms = jnp.mean(x * x, axis=-1, keepdims=True)
                o_ref[pl.ds(r0, RMS_ROWS), :] = x * lax.rsqrt(ms + NORM_EPS) * gf
                return carry

            lax.fori_loop(0, o_ref.shape[0] // RMS_ROWS, body, 0)


def _ffn(x, gain, w_in, w_out, *, tm, tf, final_gain=None):
    t, d = x.shape
    nf = FFN_DIM // tf
    in_specs = [
        pl.BlockSpec((tm, d), lambda i, f: (i, 0)),
        pl.BlockSpec((1, d), lambda i, f: (0, 0)),
        pl.BlockSpec((d, tf), lambda i, f: (0, f)),
        pl.BlockSpec((d, tf), lambda i, f: (0, nf + f)),
        pl.BlockSpec((tf, d), lambda i, f: (f, 0)),
    ]
    args = [x, gain.reshape(1, d), w_in, w_in, w_out]
    if final_gain is not None:
        in_specs.append(pl.BlockSpec((1, d), lambda i, f: (0, 0)))
        args.append(final_gain.reshape(1, d))
    return pl.pallas_call(
        functools.partial(_ffn_kernel, final_norm=final_gain is not None),
        out_shape=jax.ShapeDtypeStruct((t, d), F32),
        grid=(t // tm, nf),
        in_specs=in_specs,
        out_specs=pl.BlockSpec((tm, d), lambda i, f: (i, 0)),
        scratch_shapes=[pltpu.VMEM((tm, d), BF16)],
        compiler_params=_params(("parallel", "arbitrary")),
        name="ffn",
    )(*args)


def _t5_bucket_table(dist):
    n = np.maximum(dist, 0)
    max_exact = REL_BUCKETS // 2
    large = max_exact + (np.log(np.maximum(n, 1) / max_exact) / np.log(REL_MAX_DIST / max_exact)
                         * (REL_BUCKETS - max_exact)).astype(np.int32)
    large = np.minimum(large, REL_BUCKETS - 1)
    return np.where(n < max_exact, n, large).astype(np.int32)


def _bias_kernel(tab_ref, bucket_ref, o_ref):
    col = pl.program_id(0) * ATT_HEADS + pl.program_id(1)
    bucket = bucket_ref[0]
    acc = jnp.zeros(bucket.shape, F32)
    for b in range(REL_BUCKETS):
        acc = jnp.where(bucket == b, tab_ref[b, col], acc)
    o_ref[0, 0] = acc


def _rel_bias_tiles(rel_bias):
    blk = ATT_BLK
    i = np.arange(blk)[:, None]
    c = np.arange(2 * blk)[None, :]
    delta = blk + i - c
    buckets = np.stack([_t5_bucket_table(np.maximum(delta, 0) * d) for _, d in DIL_CONFIGS])
    return pl.pallas_call(
        _bias_kernel,
        out_shape=jax.ShapeDtypeStruct((N_GROUPS, ATT_HEADS, blk, 2 * blk), F32),
        grid=(N_GROUPS, ATT_HEADS),
        in_specs=[
            pl.BlockSpec(memory_space=pltpu.SMEM),
            pl.BlockSpec((1, blk, 2 * blk), lambda g, h: (g, 0, 0)),
        ],
        out_specs=pl.BlockSpec((1, 1, blk, 2 * blk), lambda g, h: (g, h, 0, 0)),
        compiler_params=_params(("parallel", "parallel")),
        name="rel_bias",
    )(rel_bias, jnp.asarray(buckets))


def _att_kernel(q_ref, kp_ref, kc_ref, vp_ref, vc_ref, bias_ref, o_ref, lse_ref):
    n = pl.program_id(2)
    blk = ATT_BLK
    i = lax.broadcasted_iota(jnp.int32, (blk, 2 * blk), 0)
    c = lax.broadcasted_iota(jnp.int32, (blk, 2 * blk), 1)
    delta = blk + i - c
    first = jnp.where(n > 0, 0, blk)
    valid = (delta >= 0) & (delta <= blk) & (c >= first)
    lane = lax.broadcasted_iota(jnp.int32, (blk, LANES), 1)
    lse_tile = jnp.zeros((blk, LANES), F32)
    scale = ATT_HEAD_DIM ** -0.5
    for h in range(ATT_HEADS):
        hs = slice(h * ATT_HEAD_DIM, (h + 1) * ATT_HEAD_DIM)
        q = q_ref[0, :, hs]
        kb = jnp.concatenate([kp_ref[0, :, hs], kc_ref[0, :, hs]], axis=0)
        vb = jnp.concatenate([vp_ref[0, :, hs], vc_ref[0, :, hs]], axis=0)
        s = lax.dot_general(q, kb, (((1,), (1,)), ((), ())), preferred_element_type=F32)
        s = s * scale + bias_ref[0, h]
        s = jnp.where(valid, s, NEG_INF)
        m = jnp.max(s, axis=-1, keepdims=True)
        p = jnp.exp(s - m)
        den = jnp.sum(p, axis=-1, keepdims=True)
        o_ref[0, :, hs] = jnp.dot(p.astype(BF16), vb, preferred_element_type=F32) / den
        lse_tile = jnp.where(lane == h, m + jnp.log(den), lse_tile)
    lse_ref[0] = lse_tile


def _dilated_group(q, kv, bias, gi, batch, seq):
    d = DIL_CONFIGS[gi][1]
    blk = ATT_BLK
    ln = seq // d
    nb = ln // blk
    w = ATT_WIDTH
    qv = q.reshape(batch, ln, d * N_GROUPS * w)
    kvv = kv.reshape(batch, ln, d * 2 * N_GROUPS * w)
    band = (1, blk, w)
    o, lse = pl.pallas_call(
        _att_kernel,
        out_shape=(jax.ShapeDtypeStruct((batch, ln, d * w), F32),
                   jax.ShapeDtypeStruct((batch, ln, d * LANES), F32)),
        grid=(batch, d, nb),
        in_specs=[
            pl.BlockSpec(band, lambda b, r, n: (b, n, r * N_GROUPS + gi)),
            pl.BlockSpec(band, lambda b, r, n: (b, jnp.maximum(n - 1, 0), r * 2 * N_GROUPS + gi)),
            pl.BlockSpec(band, lambda b, r, n: (b, n, r * 2 * N_GROUPS + gi)),
            pl.BlockSpec(band, lambda b, r, n: (b, jnp.maximum(n - 1, 0), r * 2 * N_GROUPS + N_GROUPS + gi)),
            pl.BlockSpec(band, lambda b, r, n: (b, n, r * 2 * N_GROUPS + N_GROUPS + gi)),
            pl.BlockSpec((1, ATT_HEADS, blk, 2 * blk), lambda b, r, n: (gi, 0, 0, 0)),
        ],
        out_specs=(pl.BlockSpec(band, lambda b, r, n: (b, n, r)),
                   pl.BlockSpec((1, blk, LANES), lambda b, r, n: (b, n, r))),
        compiler_params=_params(("parallel", "parallel", "arbitrary")),
        name="dilated_attention_g%d" % gi,
    )(qv, kvv, kvv, kvv, kvv, bias)
    return o.reshape(batch * seq, w), lse.reshape(batch * seq, LANES)


def _merge_out_kernel(o0_ref, o1_ref, o2_ref, l0_ref, l1_ref, l2_ref, x_ref, w_ref, out_ref, m_ref):
    l0 = l0_ref[...]
    l1 = l1_ref[...]
    l2 = l2_ref[...]
    mx = jnp.maximum(jnp.maximum(l0, l1), l2)
    e0 = jnp.exp(l0 - mx)
    e1 = jnp.exp(l1 - mx)
    e2 = jnp.exp(l2 - mx)
    den = e0 + e1 + e2
    w0 = e0 / den
    w1 = e1 / den
    w2 = e2 / den
    tm = l0.shape[0]
    for h in range(ATT_HEADS):
        hs = slice(h * ATT_HEAD_DIM, (h + 1) * ATT_HEAD_DIM)
        b0 = jnp.broadcast_to(w0[:, h:h + 1], (tm, ATT_HEAD_DIM))
        b1 = jnp.broadcast_to(w1[:, h:h + 1], (tm, ATT_HEAD_DIM))
        b2 = jnp.broadcast_to(w2[:, h:h + 1], (tm, ATT_HEAD_DIM))
        m_ref[:, hs] = (b0 * o0_ref[:, hs] + b1 * o1_ref[:, hs] + b2 * o2_ref[:, hs]).astype(m_ref.dtype)
    out_ref[...] = x_ref[...] + jnp.dot(m_ref[...], w_ref[...], preferred_element_type=F32)


def _merge_out(outs, lses, x, w, *, tm):
    t, d = x.shape
    wide = pl.BlockSpec((tm, ATT_WIDTH), lambda i: (i, 0))
    narrow = pl.BlockSpec((tm, LANES), lambda i: (i, 0))
    return pl.pallas_call(
        _merge_out_kernel,
        out_shape=jax.ShapeDtypeStruct((t, d), F32),
        grid=(t // tm,),
        in_specs=[wide, wide, wide, narrow, narrow, narrow,
                  pl.BlockSpec((tm, d), lambda i: (i, 0)),
                  pl.BlockSpec((ATT_WIDTH, d), lambda i: (0, 0))],
        out_specs=pl.BlockSpec((tm, d), lambda i: (i, 0)),
        scratch_shapes=[pltpu.VMEM((tm, ATT_WIDTH), BF16)],
        compiler_params=_params(("parallel",)),
        name="merge_out",
    )(*outs, *lses, x, w)


def kernel(x, g_mix, g_ffn, w_ret_in, w_ret_out, g_kv, w_kv, w_att_q, w_att_out, rel_bias, w_ffn_in, w_ffn_out, g_final):
    batch, seq, d = x.shape
    t = batch * seq
    x = x.reshape(t, d)

    inv = (1.0 / ROPE_BASE ** np.linspace(0.0, 1.0, ROPE_HALF)).astype(np.float32)
    ang = jnp.arange(seq, dtype=F32)[:, None] * jnp.asarray(inv)[None, :]
    rope = (jnp.cos(ang), jnp.sin(ang), 2 * RET_NQ, RET_NQ, RET_QK_DIM ** -0.5, seq)

    kv = bias = None
    for l in range(DEPTH):
        if l < N_RET_LAYERS:
            z = _norm_matmul(x, g_mix[l], w_ret_in[l].astype(BF16), tm=1024, tn=1024, rope=rope)
            y = _retention(z, batch, seq)
            x = _matmul_residual(y, w_ret_out[l].astype(BF16), x, tm=512, tn=1024)
        else:
            j = l - N_RET_LAYERS
            if j == 0:
                kv = _norm_matmul(x, g_kv, w_kv.astype(BF16), tm=1024, tn=1024)
                bias = _rel_bias_tiles(rel_bias)
            q = _norm_matmul(x, g_mix[l], w_att_q[j].astype(BF16), tm=1024, tn=1024)
            outs, lses = [], []
            for gi in range(N_GROUPS):
                o, lse = _dilated_group(q, kv, bias, gi, batch, seq)
                outs.append(o)
                lses.append(lse)
            x = _merge_out(outs, lses, x, w_att_out[j].astype(BF16), tm=256)
        x = _ffn(x, g_ffn[l], w_ffn_in[l].astype(BF16), w_ffn_out[l].astype(BF16), tm=512, tf=512,
                 final_gain=g_final if l == DEPTH - 1 else None)
    return x.reshape(batch, seq, d)
```

```python
import functools
from typing import NamedTuple

import numpy as np
import jax
import jax.numpy as jnp
from jax import lax
from jax.experimental import pallas as pl
from jax.experimental.pallas import tpu as pltpu

F32 = jnp.float32
BF16 = jnp.bfloat16

D_MODEL = 2048
DEPTH = 4
N_RET_LAYERS = DEPTH // 2

RET_HEADS = 8
RET_QK_DIM = D_MODEL // RET_HEADS
RET_V_DIM = 2 * RET_QK_DIM
RET_NQ = RET_HEADS * RET_QK_DIM
RET_NV = RET_HEADS * RET_V_DIM
ROPE_BASE = 10000.0
ROPE_HALF = RET_QK_DIM // 2

DIL_CONFIGS = ((128, 1), (512, 4), (2048, 16))
N_GROUPS = len(DIL_CONFIGS)
ATT_HEAD_DIM = 128
ATT_HEADS = D_MODEL // ATT_HEAD_DIM
ATT_WIDTH = ATT_HEADS * ATT_HEAD_DIM
ATT_BLK = 128
ATT_TILES_PER_STEP = 64
REL_BUCKETS = 32
REL_MAX_DIST = 2048
FFN_DIM = -(-8 * D_MODEL // (3 * 256)) * 256
NORM_EPS = 1e-6
NEG_INF = -1e30

LANES = 128
RET_CHUNK = 256
RMS_ROWS = 64
VMEM_LIMIT = 56 * 1024 * 1024


def _params(semantics):
    return pltpu.CompilerParams(dimension_semantics=semantics, vmem_limit_bytes=VMEM_LIMIT)


def _rms_rows(x_ref, g_ref, dst_ref, copy_ref=None):
    g = g_ref[...]

    def body(c, carry):
        r0 = pl.multiple_of(c * RMS_ROWS, RMS_ROWS)
        x = x_ref[pl.ds(r0, RMS_ROWS), :]
        ms = jnp.mean(x * x, axis=-1, keepdims=True)
        dst_ref[pl.ds(r0, RMS_ROWS), :] = (x * lax.rsqrt(ms + NORM_EPS) * g).astype(dst_ref.dtype)
        if copy_ref is not None:
            copy_ref[pl.ds(r0, RMS_ROWS), :] = x
        return carry

    lax.fori_loop(0, x_ref.shape[0] // RMS_ROWS, body, 0)


class _Seg(NamedTuple):
    j0: int
    nt: int
    cb0: int
    rope_scale: float


class _Out(NamedTuple):
    ncols: int
    dilation: int
    segs: tuple


def _col_block(out, j):
    cb = out.segs[0].cb0
    for seg in out.segs:
        cb = jnp.where(j >= seg.j0, seg.cb0 + jnp.minimum(j - seg.j0, seg.nt - 1), cb)
    return cb


def _norm_matmul_kernel(*refs, outs, has_rope, has_dilate):
    x_ref, g_ref, w_ref = refs[:3]
    pos = 3
    if has_rope:
        cos_ref, sin_ref = refs[3:5]
        pos = 5
    out_refs = refs[pos:pos + len(outs)]
    h_ref = refs[pos + len(outs)]
    stage_ref = refs[pos + len(outs) + 1] if has_dilate else None
    j = pl.program_id(1)

    @pl.when(j == 0)
    def _():
        _rms_rows(x_ref, g_ref, h_ref)

    acc = jnp.dot(h_ref[...], w_ref[...], preferred_element_type=F32)
    tm, tn = acc.shape

    def plain(o_ref):
        o_ref[...] = acc.astype(o_ref.dtype)

    def rope(o_ref, scale):
        cos = cos_ref[...]
        sin = sin_ref[...]
        for c in range(0, tn, 2 * ROPE_HALF):
            t1 = acc[:, c:c + ROPE_HALF] * scale
            t2 = acc[:, c + ROPE_HALF:c + 2 * ROPE_HALF] * scale
            o_ref[:, c:c + ROPE_HALF] = (t1 * cos - t2 * sin).astype(o_ref.dtype)
            o_ref[:, c + ROPE_HALF:c + 2 * ROPE_HALF] = (t2 * cos + t1 * sin).astype(o_ref.dtype)

    def dilate(o_ref, d):
        for c in range(tn // LANES):
            stage_ref[c] = acc[:, c * LANES:(c + 1) * LANES]
        for r in range(d):
            for c in range(tn // LANES):
                o_ref[0, r, :, c * LANES:(c + 1) * LANES] = (
                    stage_ref[c, pl.ds(r, tm // d, stride=d), :].astype(o_ref.dtype))

    for out, o_ref in zip(outs, out_refs):
        for seg in out.segs:
            @pl.when((j >= seg.j0) & (j < seg.j0 + seg.nt))
            def _(out=out, o_ref=o_ref, seg=seg):
                if out.dilation > 1:
                    dilate(o_ref, out.dilation)
                elif seg.rope_scale:
                    rope(o_ref, seg.rope_scale)
                else:
                    plain(o_ref)


def _norm_matmul(x, gain, w, outs, *, batch, seq, tm, tn, rope=None):
    t, d = x.shape
    n = w.shape[1]
    tiles_per_batch = seq // tm
    has_dilate = any(o.dilation > 1 for o in outs)
    in_specs = [
        pl.BlockSpec((tm, d), lambda i, j: (i, 0)),
        pl.BlockSpec((1, d), lambda i, j: (0, 0)),
        pl.BlockSpec((d, tn), lambda i, j: (0, j)),
    ]
    args = [x, gain.reshape(1, d), w]
    if rope is not None:
        in_specs += [pl.BlockSpec((tm, ROPE_HALF), lambda i, j: (i % tiles_per_batch, 0))] * 2
        args += list(rope)
    out_shapes, out_specs = [], []
    for out in outs:
        dd = out.dilation
        if dd == 1:
            out_shapes.append(jax.ShapeDtypeStruct((t, out.ncols), BF16))
            out_specs.append(pl.BlockSpec((tm, tn), lambda i, j, out=out: (i, _col_block(out, j))))
        else:
            out_shapes.append(jax.ShapeDtypeStruct((batch, dd, seq // dd, out.ncols), BF16))
            out_specs.append(pl.BlockSpec(
                (1, dd, tm // dd, tn),
                lambda i, j, out=out: (i // tiles_per_batch, 0, i % tiles_per_batch, _col_block(out, j))))
    scratch = [pltpu.VMEM((tm, d), BF16)]
    if has_dilate:
        scratch.append(pltpu.VMEM((tn // LANES, tm, LANES), F32))
    return pl.pallas_call(
        functools.partial(_norm_matmul_kernel, outs=tuple(outs), has_rope=rope is not None,
                          has_dilate=has_dilate),
        out_shape=tuple(out_shapes),
        grid=(t // tm, n // tn),
        in_specs=in_specs,
        out_specs=tuple(out_specs),
        scratch_shapes=scratch,
        compiler_params=_params(("parallel", "arbitrary")),
        name="norm_matmul",
    )(*args)


def _retention_kernel(lg_ref, cd_ref, q_ref, k_ref, v_ref, g_ref, y_ref, r_ref, dm_ref, qd_ref, kd_ref):
    h = pl.program_id(1)
    n = pl.program_id(2)
    c = q_ref.shape[0]

    @pl.when(n == 0)
    def _():
        lg = lg_ref[h]
        r_ref[...] = jnp.zeros_like(r_ref)
        i = lax.broadcasted_iota(jnp.int32, (c, c), 0)
        jj = lax.broadcasted_iota(jnp.int32, (c, c), 1)
        diff = i - jj
        dm_ref[...] = jnp.where(diff >= 0, jnp.exp(lg * jnp.maximum(diff, 0).astype(F32)), 0.0)
        row = lax.broadcasted_iota(jnp.int32, (c, LANES), 0).astype(F32)
        qd_ref[...] = jnp.exp(lg * (row + 1.0))
        kd_ref[...] = jnp.exp(lg * (F32(c - 1) - row))

    q = q_ref[...]
    k = k_ref[...]
    v = v_ref[...]
    s = lax.dot_general(q, k, (((1,), (1,)), ((), ())), preferred_element_type=F32) * dm_ref[...]
    inner = jnp.dot(s.astype(BF16), v, preferred_element_type=F32)
    r_old = r_ref[...]
    cross = jnp.dot(q, r_old.astype(BF16), preferred_element_type=F32)
    qd = qd_ref[...]
    kd = kd_ref[...]
    dk = k.shape[1]
    k_dec = jnp.concatenate(
        [(k[:, o:o + LANES].astype(F32) * kd).astype(BF16) for o in range(0, dk, LANES)], axis=1)
    r_ref[...] = r_old * cd_ref[h] + lax.dot_general(
        k_dec, v, (((0,), (0,)), ((), ())), preferred_element_type=F32)

    dv = v.shape[1]
    y = jnp.concatenate(
        [inner[:, o:o + LANES] + cross[:, o:o + LANES] * qd for o in range(0, dv, LANES)], axis=1)
    ms = jnp.mean(y * y, axis=-1, keepdims=True)
    g = g_ref[...].astype(F32)
    gate = g * (1.0 / (1.0 + jnp.exp(-g)))
    y_ref[...] = ((y * lax.rsqrt(ms + NORM_EPS)) * gate).astype(y_ref.dtype)


def _retention(q, k, v, g, batch, seq):
    t = q.shape[0]
    c = RET_CHUNK
    nc = seq // c
    log_g = np.log(1.0 - 2.0 ** (-5.0 - np.arange(RET_HEADS))).astype(np.float32)
    c_dec = np.exp(log_g * c).astype(np.float32)
    smem = pl.BlockSpec(memory_space=pltpu.SMEM)
    qk_spec = pl.BlockSpec((c, RET_QK_DIM), lambda b, h, n: (b * nc + n, h))
    v_spec = pl.BlockSpec((c, RET_V_DIM), lambda b, h, n: (b * nc + n, h))
    return pl.pallas_call(
        _retention_kernel,
        out_shape=jax.ShapeDtypeStruct((t, RET_NV), BF16),
        grid=(batch, RET_HEADS, nc),
        in_specs=[smem, smem, qk_spec, qk_spec, v_spec, v_spec],
        out_specs=v_spec,
        scratch_shapes=[
            pltpu.VMEM((RET_QK_DIM, RET_V_DIM), F32),
            pltpu.VMEM((c, c), F32),
            pltpu.VMEM((c, LANES), F32),
            pltpu.VMEM((c, LANES), F32),
        ],
        compiler_params=_params(("parallel", "parallel", "arbitrary")),
        name="retention",
    )(jnp.asarray(log_g), jnp.asarray(c_dec), q, k, v, g)


def _matmul_residual_kernel(y_ref, w_ref, x_ref, o_ref):
    o_ref[...] = x_ref[...] + jnp.dot(y_ref[...], w_ref[...], preferred_element_type=F32)


def _matmul_residual(y, w, x, *, tm, tn):
    t, k = y.shape
    n = w.shape[1]
    return pl.pallas_call(
        _matmul_residual_kernel,
        out_shape=jax.ShapeDtypeStruct((t, n), F32),
        grid=(n // tn, t // tm),
        in_specs=[
            pl.BlockSpec((tm, k), lambda j, i: (i, 0)),
            pl.BlockSpec((k, tn), lambda j, i: (0, j)),
            pl.BlockSpec((tm, tn), lambda j, i: (i, j)),
        ],
        out_specs=pl.BlockSpec((tm, tn), lambda j, i: (i, j)),
        compiler_params=_params(("parallel", "parallel")),
        name="matmul_residual",
    )(y, w, x)


def _ffn_kernel(*refs, final_norm):
    if final_norm:
        x_ref, g_ref, w1_ref, w2_ref, wo_ref, gf_ref, o_ref, h_ref = refs
    else:
        x_ref, g_ref, w1_ref, w2_ref, wo_ref, o_ref, h_ref = refs
    f = pl.program_id(1)

    @pl.when(f == 0)
    def _():
        _rms_rows(x_ref, g_ref, h_ref, copy_ref=o_ref)

    h = h_ref[...]
    z1 = jnp.dot(h, w1_ref[...], preferred_element_type=F32)
    z2 = jnp.dot(h, w2_ref[...], preferred_element_type=F32)
    a = (z1 * (1.0 / (1.0 + jnp.exp(-z1)))) * z2
    o_ref[...] += jnp.dot(a.astype(BF16), wo_ref[...], preferred_element_type=F32)

    if final_norm:
        @pl.when(f == pl.num_programs(1) - 1)
        def _():
            gf = gf_ref[...]

            def body(c, carry):
                r0 = pl.multiple_of(c * RMS_ROWS, RMS_ROWS)
                x = o_ref[pl.ds(r0, RMS_ROWS), :]
                ms = jnp.mean(x * x, axis=-1, keepdims=True)
                o_ref[pl.ds(r0, RMS_ROWS), :] = x * lax.rsqrt(ms + NORM_EPS) * gf
                return carry

            lax.fori_loop(0, o_ref.shape[0] // RMS_ROWS, body, 0)


def _ffn(x, gain, w_in, w_out, *, tm, tf, final_gain=None):
    t, d = x.shape
    nf = FFN_DIM // tf
    in_specs = [
        pl.BlockSpec((tm, d), lambda i, f: (i, 0)),
        pl.BlockSpec((1, d), lambda i, f: (0, 0)),
        pl.BlockSpec((d, tf), lambda i, f: (0, f)),
        pl.BlockSpec((d, tf), lambda i, f: (0, nf + f)),
        pl.BlockSpec((tf, d), lambda i, f: (f, 0)),
    ]
    args = [x, gain.reshape(1, d), w_in, w_in, w_out]
    if final_gain is not None:
        in_specs.append(pl.BlockSpec((1, d), lambda i, f: (0, 0)))
        args.append(final_gain.reshape(1, d))
    return pl.pallas_call(
        functools.partial(_ffn_kernel, final_norm=final_gain is not None),
        out_shape=jax.ShapeDtypeStruct((t, d), F32),
        grid=(t // tm, nf),
        in_specs=in_specs,
        out_specs=pl.BlockSpec((tm, d), lambda i, f: (i, 0)),
        scratch_shapes=[pltpu.VMEM((tm, d), BF16)],
        compiler_params=_params(("parallel", "arbitrary")),
        name="ffn",
    )(*args)


def _t5_bucket_table(dist):
    n = np.maximum(dist, 0)
    max_exact = REL_BUCKETS // 2
    large = max_exact + (np.log(np.maximum(n, 1) / max_exact) / np.log(REL_MAX_DIST / max_exact)
                         * (REL_BUCKETS - max_exact)).astype(np.int32)
    large = np.minimum(large, REL_BUCKETS - 1)
    return np.where(n < max_exact, n, large).astype(np.int32)


def _bias_kernel(tab_ref, bucket_ref, o_ref):
    col = pl.program_id(0) * ATT_HEADS + pl.program_id(1)
    bucket = bucket_ref[0]
    acc = jnp.zeros(bucket.shape, F32)
    for b in range(REL_BUCKETS):
        acc = jnp.where(bucket == b, tab_ref[b, col], acc)
    o_ref[0, 0] = acc


def _rel_bias_tiles(rel_bias):
    blk = ATT_BLK
    i = np.arange(blk)[:, None]
    c = np.arange(2 * blk)[None, :]
    delta = blk + i - c
    buckets = np.stack([_t5_bucket_table(np.maximum(delta, 0) * d) for _, d in DIL_CONFIGS])
    return pl.pallas_call(
        _bias_kernel,
        out_shape=jax.ShapeDtypeStruct((N_GROUPS, ATT_HEADS, blk, 2 * blk), F32),
        grid=(N_GROUPS, ATT_HEADS),
        in_specs=[
            pl.BlockSpec(memory_space=pltpu.SMEM),
            pl.BlockSpec((1, blk, 2 * blk), lambda g, h: (g, 0, 0)),
        ],
        out_specs=pl.BlockSpec((1, 1, blk, 2 * blk), lambda g, h: (g, h, 0, 0)),
        compiler_params=_params(("parallel", "parallel")),
        name="rel_bias",
    )(rel_bias, jnp.asarray(buckets))


def _att_kernel(q_ref, kp_ref, kc_ref, vp_ref, vc_ref, bias_ref, o_ref, lse_ref, *, d, hpb):
    n = pl.program_id(1)
    hb = pl.program_id(2)
    blk = ATT_BLK
    i = lax.broadcasted_iota(jnp.int32, (blk, 2 * blk), 0)
    c = lax.broadcasted_iota(jnp.int32, (blk, 2 * blk), 1)
    delta = blk + i - c
    first = jnp.where(n > 0, 0, blk)
    valid = (delta >= 0) & (delta <= blk) & (c >= first)
    lane = lax.broadcasted_iota(jnp.int32, (blk, LANES), 1)
    scale = ATT_HEAD_DIM ** -0.5
    revisited = hpb < ATT_HEADS

    if revisited:
        @pl.when(hb == 0)
        def _():
            lse_ref[...] = jnp.zeros_like(lse_ref)

    def residue(r, carry):
        rows = pl.ds(r, blk, stride=d) if d > 1 else slice(None)
        lse_tile = lse_ref[rows, :] if revisited else jnp.zeros((blk, LANES), F32)
        for hl in range(hpb):
            hs = slice(hl * ATT_HEAD_DIM, (hl + 1) * ATT_HEAD_DIM)
            q = q_ref[0, r, :, hs]
            kb = jnp.concatenate([kp_ref[0, r, :, hs], kc_ref[0, r, :, hs]], axis=0)
            vb = jnp.concatenate([vp_ref[0, r, :, hs], vc_ref[0, r, :, hs]], axis=0)
            s = lax.dot_general(q, kb, (((1,), (1,)), ((), ())), preferred_element_type=F32)
            s = s * scale + bias_ref[0, hl]
            s = jnp.where(valid, s, NEG_INF)
            m = jnp.max(s, axis=-1, keepdims=True)
            p = jnp.exp(s - m)
            den = jnp.sum(p, axis=-1, keepdims=True)
            o_ref[hl, rows, :] = jnp.dot(p.astype(BF16), vb, preferred_element_type=F32) / den
            lse_tile = jnp.where(lane == hb * hpb + hl, m + jnp.log(den), lse_tile)
        lse_ref[rows, :] = lse_tile
        return carry

    if d == 1:
        residue(0, 0)
    else:
        lax.fori_loop(0, d, residue, 0)


def _dilated_group(q, kv, bias, gi, batch, seq):
    d = DIL_CONFIGS[gi][1]
    blk = ATT_BLK
    nb = seq // d // blk
    hpb = min(ATT_HEADS, ATT_TILES_PER_STEP // d)
    nhb = ATT_HEADS // hpb
    band = (1, d, blk, hpb * ATT_HEAD_DIM)
    prev = lambda n: jnp.maximum(n - 1, 0)
    return pl.pallas_call(
        functools.partial(_att_kernel, d=d, hpb=hpb),
        out_shape=(jax.ShapeDtypeStruct((ATT_HEADS, batch * seq, ATT_HEAD_DIM), F32),
                   jax.ShapeDtypeStruct((batch * seq, LANES), F32)),
        grid=(batch, nb, nhb),
        in_specs=[
            pl.BlockSpec(band, lambda b, n, hb: (b, 0, n, hb)),
            pl.BlockSpec(band, lambda b, n, hb: (b, 0, prev(n), hb)),
            pl.BlockSpec(band, lambda b, n, hb: (b, 0, n, hb)),
            pl.BlockSpec(band, lambda b, n, hb: (b, 0, prev(n), nhb + hb)),
            pl.BlockSpec(band, lambda b, n, hb: (b, 0, n, nhb + hb)),
            pl.BlockSpec((1, hpb, blk, 2 * blk), lambda b, n, hb: (gi, hb, 0, 0)),
        ],
        out_specs=(pl.BlockSpec((hpb, d * blk, ATT_HEAD_DIM), lambda b, n, hb: (hb, b * nb + n, 0)),
                   pl.BlockSpec((d * blk, LANES), lambda b, n, hb: (b * nb + n, 0))),
        compiler_params=_params(("parallel", "parallel", "arbitrary")),
        name="dilated_attention_g%d" % gi,
    )(q, kv, kv, kv, kv, bias)


def _merge_out_kernel(o0_ref, o1_ref, o2_ref, l0_ref, l1_ref, l2_ref, x_ref, w_ref, out_ref, m_ref):
    l0 = l0_ref[...]
    l1 = l1_ref[...]
    l2 = l2_ref[...]
    mx = jnp.maximum(jnp.maximum(l0, l1), l2)
    e0 = jnp.exp(l0 - mx)
    e1 = jnp.exp(l1 - mx)
    e2 = jnp.exp(l2 - mx)
    den = e0 + e1 + e2
    w0 = e0 / den
    w1 = e1 / den
    w2 = e2 / den
    tm = l0.shape[0]
    for h in range(ATT_HEADS):
        hs = slice(h * ATT_HEAD_DIM, (h + 1) * ATT_HEAD_DIM)
        b0 = jnp.broadcast_to(w0[:, h:h + 1], (tm, ATT_HEAD_DIM))
        b1 = jnp.broadcast_to(w1[:, h:h + 1], (tm, ATT_HEAD_DIM))
        b2 = jnp.broadcast_to(w2[:, h:h + 1], (tm, ATT_HEAD_DIM))
        m_ref[:, hs] = (b0 * o0_ref[h] + b1 * o1_ref[h] + b2 * o2_ref[h]).astype(m_ref.dtype)
    out_ref[...] = x_ref[...] + jnp.dot(m_ref[...], w_ref[...], preferred_element_type=F32)


def _merge_out(outs, lses, x, w, *, tm):
    t, d = x.shape
    heads = pl.BlockSpec((ATT_HEADS, tm, ATT_HEAD_DIM), lambda i: (0, i, 0))
    narrow = pl.BlockSpec((tm, LANES), lambda i: (i, 0))
    return pl.pallas_call(
        _merge_out_kernel,
        out_shape=jax.ShapeDtypeStruct((t, d), F32),
        grid=(t // tm,),
        in_specs=[heads, heads, heads, narrow, narrow, narrow,
                  pl.BlockSpec((tm, d), lambda i: (i, 0)),
                  pl.BlockSpec((ATT_WIDTH, d), lambda i: (0, 0))],
        out_specs=pl.BlockSpec((tm, d), lambda i: (i, 0)),
        scratch_shapes=[pltpu.VMEM((tm, ATT_WIDTH), BF16)],
        compiler_params=_params(("parallel",)),
        name="merge_out",
    )(*outs, *lses, x, w)


def _as_residue_major(a, batch, seq):
    return a.reshape(batch, 1, seq, a.shape[1]) if a.ndim == 2 else a


def kernel(x, g_mix, g_ffn, w_ret_in, w_ret_out, g_kv, w_kv, w_att_q, w_att_out, rel_bias, w_ffn_in, w_ffn_out, g_final):
    batch, seq, d = x.shape
    t = batch * seq
    x = x.reshape(t, d)
    tm, tn = 512, 1024
    dims = dict(batch=batch, seq=seq, tm=tm, tn=tn)

    inv = (1.0 / ROPE_BASE ** np.linspace(0.0, 1.0, ROPE_HALF)).astype(np.float32)
    ang = jnp.arange(seq, dtype=F32)[:, None] * jnp.asarray(inv)[None, :]
    rope = (jnp.cos(ang), jnp.sin(ang))

    qt, vt, wt = RET_NQ // tn, RET_NV // tn, ATT_WIDTH // tn
    ret_outs = (
        _Out(RET_NQ, 1, (_Seg(0, qt, 0, 1.0),)),
        _Out(RET_NQ, 1, (_Seg(qt, qt, 0, RET_QK_DIM ** -0.5),)),
        _Out(RET_NV, 1, (_Seg(2 * qt, vt, 0, 0.0),)),
        _Out(RET_NV, 1, (_Seg(2 * qt + vt, vt, 0, 0.0),)),
    )
    kv_outs = tuple(
        _Out(2 * ATT_WIDTH, dd, (_Seg(gi * wt, wt, 0, 0.0), _Seg((N_GROUPS + gi) * wt, wt, wt, 0.0)))
        for gi, (_, dd) in enumerate(DIL_CONFIGS))
    q_outs = tuple(
        _Out(ATT_WIDTH, dd, (_Seg(gi * wt, wt, 0, 0.0),)) for gi, (_, dd) in enumerate(DIL_CONFIGS))

    kvs = bias = None
    for l in range(DEPTH):
        if l < N_RET_LAYERS:
            q, k, v, g = _norm_matmul(x, g_mix[l], w_ret_in[l].astype(BF16), ret_outs, rope=rope, **dims)
            y = _retention(q, k, v, g, batch, seq)
            x = _matmul_residual(y, w_ret_out[l].astype(BF16), x, tm=512, tn=1024)
        else:
            j = l - N_RET_LAYERS
            if j == 0:
                kvs = _norm_matmul(x, g_kv, w_kv.astype(BF16), kv_outs, **dims)
                bias = _rel_bias_tiles(rel_bias)
            qs = _norm_matmul(x, g_mix[l], w_att_q[j].astype(BF16), q_outs, **dims)
            outs, lses = [], []
            for gi in range(N_GROUPS):
                o, lse = _dilated_group(_as_residue_major(qs[gi], batch, seq),
                                        _as_residue_major(kvs[gi], batch, seq), bias, gi, batch, seq)
                outs.append(o)
                lses.append(lse)
            x = _merge_out(outs, lses, x, w_att_out[j].astype(BF16), tm=256)
        x = _ffn(x, g_ffn[l], w_ffn_in[l].astype(BF16), w_ffn_out[l].astype(BF16), tm=512, tf=512,
                 final_gain=g_final if l == DEPTH - 1 else None)
    return x.reshape(batch, seq, d)
```

```python
import functools
from typing import NamedTuple

import numpy as np
import jax
import jax.numpy as jnp
from jax import lax
from jax.experimental import pallas as pl
from jax.experimental.pallas import tpu as pltpu

F32 = jnp.float32
BF16 = jnp.bfloat16

D_MODEL = 2048
DEPTH = 4
N_RET_LAYERS = DEPTH // 2

RET_HEADS = 8
RET_QK_DIM = D_MODEL // RET_HEADS
RET_V_DIM = 2 * RET_QK_DIM
RET_NQ = RET_HEADS * RET_QK_DIM
RET_NV = RET_HEADS * RET_V_DIM
ROPE_BASE = 10000.0
ROPE_HALF = RET_QK_DIM // 2

DIL_CONFIGS = ((128, 1), (512, 4), (2048, 16))
N_GROUPS = len(DIL_CONFIGS)
ATT_HEAD_DIM = 128
ATT_HEADS = D_MODEL // ATT_HEAD_DIM
ATT_WIDTH = ATT_HEADS * ATT_HEAD_DIM
ATT_BLK = 128
ATT_TILES_PER_STEP = 64
REL_BUCKETS = 32
REL_MAX_DIST = 2048
FFN_DIM = -(-8 * D_MODEL // (3 * 256)) * 256
NORM_EPS = 1e-6
NEG_INF = -1e30

LANES = 128
RET_CHUNK = 256
RMS_ROWS = 64
VMEM_LIMIT = 56 * 1024 * 1024


def _params(semantics):
    return pltpu.CompilerParams(dimension_semantics=semantics, vmem_limit_bytes=VMEM_LIMIT)


def _rms_rows(x_ref, g_ref, dst_ref, copy_ref=None, stage_ref=None):
    g = g_ref[...]

    def body(c, carry):
        r0 = pl.multiple_of(c * RMS_ROWS, RMS_ROWS)
        x = x_ref[pl.ds(r0, RMS_ROWS), :]
        ms = jnp.mean(x * x, axis=-1, keepdims=True)
        y = x * lax.rsqrt(ms + NORM_EPS) * g
        dst_ref[pl.ds(r0, RMS_ROWS), :] = y.astype(dst_ref.dtype)
        if copy_ref is not None:
            copy_ref[pl.ds(r0, RMS_ROWS), :] = x
        if stage_ref is not None:
            for k in range(stage_ref.shape[0]):
                stage_ref[k, pl.ds(r0, RMS_ROWS), :] = y[:, k * LANES:(k + 1) * LANES]
        return carry

    lax.fori_loop(0, x_ref.shape[0] // RMS_ROWS, body, 0)


class _Seg(NamedTuple):
    j0: int
    nt: int
    cb0: int
    scale: float


class _Out(NamedTuple):
    ncols: int
    dilation: int
    rope: bool
    segs: tuple


def _col_block(out, j):
    cb = out.segs[0].cb0
    for seg in out.segs:
        cb = jnp.where(j >= seg.j0, seg.cb0 + jnp.minimum(j - seg.j0, seg.nt - 1), cb)
    return cb


def _norm_matmul_kernel(*refs, outs, has_rope, dilations):
    x_ref, g_ref, w_ref = refs[:3]
    pos = 3
    if has_rope:
        cos_ref, sin_ref = refs[3:5]
        pos = 5
    out_refs = refs[pos:pos + len(outs)]
    pos += len(outs)
    h_refs = {1: refs[pos]}
    for k, d in enumerate(dilations):
        h_refs[d] = refs[pos + 1 + k]
    stage_ref = refs[pos + 1 + len(dilations)] if dilations else None
    j = pl.program_id(1)
    tm = x_ref.shape[0]

    @pl.when(j == 0)
    def _():
        _rms_rows(x_ref, g_ref, h_refs[1], stage_ref=stage_ref)
        for d in dilations:
            n_r = tm // d
            for r in range(d):
                for k in range(stage_ref.shape[0]):
                    h_refs[d][r * n_r:(r + 1) * n_r, k * LANES:(k + 1) * LANES] = (
                        stage_ref[k, pl.ds(r, n_r, stride=d), :].astype(BF16))

    def emit(out, o_ref):
        acc = jnp.dot(h_refs[out.dilation][...], w_ref[...], preferred_element_type=F32)
        tn = acc.shape[1]
        if out.dilation > 1:
            n_r = tm // out.dilation
            for r in range(out.dilation):
                o_ref[0, r] = acc[r * n_r:(r + 1) * n_r, :].astype(o_ref.dtype)
        elif out.rope:
            scale = F32(out.segs[0].scale)
            for seg in out.segs[1:]:
                scale = jnp.where(j >= seg.j0, F32(seg.scale), scale)
            cos = cos_ref[...]
            sin = sin_ref[...]
            for c in range(0, tn, 2 * ROPE_HALF):
                t1 = acc[:, c:c + ROPE_HALF] * scale
                t2 = acc[:, c + ROPE_HALF:c + 2 * ROPE_HALF] * scale
                o_ref[:, c:c + ROPE_HALF] = (t1 * cos - t2 * sin).astype(o_ref.dtype)
                o_ref[:, c + ROPE_HALF:c + 2 * ROPE_HALF] = (t2 * cos + t1 * sin).astype(o_ref.dtype)
        else:
            o_ref[...] = acc.astype(o_ref.dtype)

    for out, o_ref in zip(outs, out_refs):
        hit = None
        for seg in out.segs:
            inside = (j >= seg.j0) & (j < seg.j0 + seg.nt)
            hit = inside if hit is None else hit | inside
        pl.when(hit)(functools.partial(emit, out, o_ref))


def _norm_matmul(x, gain, w, outs, *, batch, seq, tm, tn, rope=None):
    t, d = x.shape
    n = w.shape[1]
    tiles_per_batch = seq // tm
    dilations = tuple(sorted({o.dilation for o in outs} - {1}))
    in_specs = [
        pl.BlockSpec((tm, d), lambda i, j: (i, 0)),
        pl.BlockSpec((1, d), lambda i, j: (0, 0)),
        pl.BlockSpec((d, tn), lambda i, j: (0, j)),
    ]
    args = [x, gain.reshape(1, d), w]
    if rope is not None:
        in_specs += [pl.BlockSpec((tm, ROPE_HALF), lambda i, j: (i % tiles_per_batch, 0))] * 2
        args += list(rope)
    out_shapes, out_specs = [], []
    for out in outs:
        dd = out.dilation
        if dd == 1:
            out_shapes.append(jax.ShapeDtypeStruct((t, out.ncols), BF16))
            out_specs.append(pl.BlockSpec((tm, tn), lambda i, j, out=out: (i, _col_block(out, j))))
        else:
            out_shapes.append(jax.ShapeDtypeStruct((batch, dd, seq // dd, out.ncols), BF16))
            out_specs.append(pl.BlockSpec(
                (1, dd, tm // dd, tn),
                lambda i, j, out=out: (i // tiles_per_batch, 0, i % tiles_per_batch, _col_block(out, j))))
    scratch = [pltpu.VMEM((tm, d), BF16) for _ in range(1 + len(dilations))]
    if dilations:
        scratch.append(pltpu.VMEM((d // LANES, tm, LANES), F32))
    return pl.pallas_call(
        functools.partial(_norm_matmul_kernel, outs=tuple(outs), has_rope=rope is not None,
                          dilations=dilations),
        out_shape=tuple(out_shapes),
        grid=(t // tm, n // tn),
        in_specs=in_specs,
        out_specs=tuple(out_specs),
        scratch_shapes=scratch,
        compiler_params=_params(("parallel", "arbitrary")),
        name="norm_matmul",
    )(*args)


def _retention_kernel(lg_ref, cd_ref, q_ref, k_ref, v_ref, g_ref, y_ref, r_ref, dm_ref, qd_ref, kd_ref, *, hps):
    hp = pl.program_id(1)
    n = pl.program_id(2)
    c = q_ref.shape[0]
    dk, dv = RET_QK_DIM, RET_V_DIM

    @pl.when(n == 0)
    def _():
        r_ref[...] = jnp.zeros_like(r_ref)
        i = lax.broadcasted_iota(jnp.int32, (c, c), 0)
        jj = lax.broadcasted_iota(jnp.int32, (c, c), 1)
        diff = i - jj
        row = lax.broadcasted_iota(jnp.int32, (c, LANES), 0).astype(F32)
        for hh in range(hps):
            lg = lg_ref[hp * hps + hh]
            dm_ref[hh] = jnp.where(diff >= 0, jnp.exp(lg * jnp.maximum(diff, 0).astype(F32)), 0.0)
            qd_ref[hh] = jnp.exp(lg * (row + 1.0))
            kd_ref[hh] = jnp.exp(lg * (F32(c - 1) - row))

    for hh in range(hps):
        q = q_ref[:, hh * dk:(hh + 1) * dk]
        k = k_ref[:, hh * dk:(hh + 1) * dk]
        v = v_ref[:, hh * dv:(hh + 1) * dv]
        s = lax.dot_general(q, k, (((1,), (1,)), ((), ())), preferred_element_type=F32) * dm_ref[hh]
        inner = jnp.dot(s.astype(BF16), v, preferred_element_type=F32)
        r_old = r_ref[hh]
        cross = jnp.dot(q, r_old.astype(BF16), preferred_element_type=F32)
        qd = qd_ref[hh]
        kd = kd_ref[hh]
        k_dec = jnp.concatenate(
            [(k[:, o:o + LANES].astype(F32) * kd).astype(BF16) for o in range(0, dk, LANES)], axis=1)
        r_ref[hh] = r_old * cd_ref[hp * hps + hh] + lax.dot_general(
            k_dec, v, (((0,), (0,)), ((), ())), preferred_element_type=F32)

        y = jnp.concatenate(
            [inner[:, o:o + LANES] + cross[:, o:o + LANES] * qd for o in range(0, dv, LANES)], axis=1)
        ms = jnp.mean(y * y, axis=-1, keepdims=True)
        g = g_ref[:, hh * dv:(hh + 1) * dv].astype(F32)
        gate = g * (1.0 / (1.0 + jnp.exp(-g)))
        y_ref[:, hh * dv:(hh + 1) * dv] = ((y * lax.rsqrt(ms + NORM_EPS)) * gate).astype(y_ref.dtype)


def _retention(qk, vg, batch, seq, *, hps):
    t = qk.shape[0]
    c = RET_CHUNK
    nc = seq // c
    nhp = RET_HEADS // hps
    log_g = np.log(1.0 - 2.0 ** (-5.0 - np.arange(RET_HEADS))).astype(np.float32)
    c_dec = np.exp(log_g * c).astype(np.float32)
    smem = pl.BlockSpec(memory_space=pltpu.SMEM)
    qk_shape = (c, hps * RET_QK_DIM)
    v_shape = (c, hps * RET_V_DIM)
    return pl.pallas_call(
        functools.partial(_retention_kernel, hps=hps),
        out_shape=jax.ShapeDtypeStruct((t, RET_NV), BF16),
        grid=(batch, nhp, nc),
        in_specs=[
            smem, smem,
            pl.BlockSpec(qk_shape, lambda b, h, n: (b * nc + n, h)),
            pl.BlockSpec(qk_shape, lambda b, h, n: (b * nc + n, nhp + h)),
            pl.BlockSpec(v_shape, lambda b, h, n: (b * nc + n, h)),
            pl.BlockSpec(v_shape, lambda b, h, n: (b * nc + n, nhp + h)),
        ],
        out_specs=pl.BlockSpec(v_shape, lambda b, h, n: (b * nc + n, h)),
        scratch_shapes=[
            pltpu.VMEM((hps, RET_QK_DIM, RET_V_DIM), F32),
            pltpu.VMEM((hps, c, c), F32),
            pltpu.VMEM((hps, c, LANES), F32),
            pltpu.VMEM((hps, c, LANES), F32),
        ],
        compiler_params=_params(("parallel", "parallel", "arbitrary")),
        name="retention",
    )(jnp.asarray(log_g), jnp.asarray(c_dec), qk, qk, vg, vg)


def _matmul_residual_kernel(y_ref, w_ref, x_ref, o_ref):
    o_ref[...] = x_ref[...] + jnp.dot(y_ref[...], w_ref[...], preferred_element_type=F32)


def _matmul_residual(y, w, x, *, tm, tn):
    t, k = y.shape
    n = w.shape[1]
    return pl.pallas_call(
        _matmul_residual_kernel,
        out_shape=jax.ShapeDtypeStruct((t, n), F32),
        grid=(n // tn, t // tm),
        in_specs=[
            pl.BlockSpec((tm, k), lambda j, i: (i, 0)),
            pl.BlockSpec((k, tn), lambda j, i: (0, j)),
            pl.BlockSpec((tm, tn), lambda j, i: (i, j)),
        ],
        out_specs=pl.BlockSpec((tm, tn), lambda j, i: (i, j)),
        compiler_params=_params(("parallel", "parallel")),
        name="matmul_residual",
    )(y, w, x)


def _ffn_kernel(*refs, final_norm):
    if final_norm:
        x_ref, g_ref, w1_ref, w2_ref, wo_ref, gf_ref, o_ref, h_ref = refs
    else:
        x_ref, g_ref, w1_ref, w2_ref, wo_ref, o_ref, h_ref = refs
    f = pl.program_id(1)

    @pl.when(f == 0)
    def _():
        _rms_rows(x_ref, g_ref, h_ref, copy_ref=o_ref)

    h = h_ref[...]
    z1 = jnp.dot(h, w1_ref[...], preferred_element_type=F32)
    z2 = jnp.dot(h, w2_ref[...], preferred_element_type=F32)
    a = (z1 * (1.0 / (1.0 + jnp.exp(-z1)))) * z2
    o_ref[...] += jnp.dot(a.astype(BF16), wo_ref[...], preferred_element_type=F32)

    if final_norm:
        @pl.when(f == pl.num_programs(1) - 1)
        def _():
            gf = gf_ref[...]

            def body(c, carry):
                r0 = pl.multiple_of(c * RMS_ROWS, RMS_ROWS)
                x = o_ref[pl.ds(r0, RMS_ROWS), :]
                ms = jnp.mean(x * x, axis=-1, keepdims=True)
                o_ref[pl.ds(r0, RMS_ROWS), :] = x * lax.rsqrt(ms + NORM_EPS) * gf
                return carry

            lax.fori_loop(0, o_ref.shape[0] // RMS_ROWS, body, 0)


def _ffn(x, gain, w_in, w_out, *, tm, tf, final_gain=None):
    t, d = x.shape
    nf = FFN_DIM // tf
    in_specs = [
        pl.BlockSpec((tm, d), lambda i, f: (i, 0)),
        pl.BlockSpec((1, d), lambda i, f: (0, 0)),
        pl.BlockSpec((d, tf), lambda i, f: (0, f)),
        pl.BlockSpec((d, tf), lambda i, f: (0, nf + f)),
        pl.BlockSpec((tf, d), lambda i, f: (f, 0)),
    ]
    args = [x, gain.reshape(1, d), w_in, w_in, w_out]
    if final_gain is not None:
        in_specs.append(pl.BlockSpec((1, d), lambda i, f: (0, 0)))
        args.append(final_gain.reshape(1, d))
    return pl.pallas_call(
        functools.partial(_ffn_kernel, final_norm=final_gain is not None),
        out_shape=jax.ShapeDtypeStruct((t, d), F32),
        grid=(t // tm, nf),
        in_specs=in_specs,
        out_specs=pl.BlockSpec((tm, d), lambda i, f: (i, 0)),
        scratch_shapes=[pltpu.VMEM((tm, d), BF16)],
        compiler_params=_params(("parallel", "arbitrary")),
        name="ffn",
    )(*args)


def _t5_bucket_table(dist):
    n = np.maximum(dist, 0)
    max_exact = REL_BUCKETS // 2
    large = max_exact + (np.log(np.maximum(n, 1) / max_exact) / np.log(REL_MAX_DIST / max_exact)
                         * (REL_BUCKETS - max_exact)).astype(np.int32)
    large = np.minimum(large, REL_BUCKETS - 1)
    return np.where(n < max_exact, n, large).astype(np.int32)


def _bias_kernel(tab_ref, bucket_ref, o_ref):
    col = pl.program_id(0) * ATT_HEADS + pl.program_id(1)
    bucket = bucket_ref[0]
    acc = jnp.zeros(bucket.shape, F32)
    for b in range(REL_BUCKETS):
        acc = jnp.where(bucket == b, tab_ref[b, col], acc)
    o_ref[0, 0] = acc


def _rel_bias_tiles(rel_bias):
    blk = ATT_BLK
    i = np.arange(blk)[:, None]
    c = np.arange(2 * blk)[None, :]
    delta = blk + i - c
    buckets = np.stack([_t5_bucket_table(np.maximum(delta, 0) * d) for _, d in DIL_CONFIGS])
    return pl.pallas_call(
        _bias_kernel,
        out_shape=jax.ShapeDtypeStruct((N_GROUPS, ATT_HEADS, blk, 2 * blk), F32),
        grid=(N_GROUPS, ATT_HEADS),
        in_specs=[
            pl.BlockSpec(memory_space=pltpu.SMEM),
            pl.BlockSpec((1, blk, 2 * blk), lambda g, h: (g, 0, 0)),
        ],
        out_specs=pl.BlockSpec((1, 1, blk, 2 * blk), lambda g, h: (g, h, 0, 0)),
        compiler_params=_params(("parallel", "parallel")),
        name="rel_bias",
    )(rel_bias, jnp.asarray(buckets))


def _att_kernel(q_ref, kp_ref, kc_ref, vp_ref, vc_ref, bias_ref, cap_ref, o_ref, lse_ref, *, d, hpb):
    n = pl.program_id(1)
    hb = pl.program_id(2)
    blk = ATT_BLK
    lane = lax.broadcasted_iota(jnp.int32, (blk, LANES), 1)
    scale = ATT_HEAD_DIM ** -0.5
    first = jnp.where(n > 0, 0, 1)

    def residue(r, carry):
        rows = pl.ds(r, blk, stride=d) if d > 1 else slice(None)
        lse_tile = jnp.zeros((blk, LANES), F32)
        for hl in range(hpb):
            hs = slice(hl * ATT_HEAD_DIM, (hl + 1) * ATT_HEAD_DIM)
            q = q_ref[0, r, :, hs]
            kb = jnp.concatenate([kp_ref[0, r, :, hs], kc_ref[0, r, :, hs]], axis=0)
            vb = jnp.concatenate([vp_ref[0, r, :, hs], vc_ref[0, r, :, hs]], axis=0)
            s = lax.dot_general(q, kb, (((1,), (1,)), ((), ())), preferred_element_type=F32)
            s = jnp.minimum(s * scale + bias_ref[0, hl], cap_ref[first])
            m = jnp.max(s, axis=-1, keepdims=True)
            p = jnp.exp(s - m)
            den = jnp.sum(p, axis=-1, keepdims=True)
            o_ref[hl, rows, :] = jnp.dot(p.astype(BF16), vb, preferred_element_type=F32) / den
            lse_tile = jnp.where(lane == hb * hpb + hl, m + jnp.log(den), lse_tile)
        lse_ref[0, rows, :] = lse_tile
        return carry

    if d == 1:
        residue(0, 0)
    else:
        lax.fori_loop(0, d, residue, 0)


def _band_caps():
    blk = ATT_BLK
    i = np.arange(blk)[:, None]
    c = np.arange(2 * blk)[None, :]
    delta = blk + i - c
    band = (delta >= 0) & (delta <= blk)
    caps = np.stack([band, band & (c >= blk)])
    return np.where(caps, np.finfo(np.float32).max, NEG_INF).astype(np.float32)


def _dilated_group(q, kv, bias, gi, batch, seq):
    d = DIL_CONFIGS[gi][1]
    blk = ATT_BLK
    nb = seq // d // blk
    hpb = min(ATT_HEADS, ATT_TILES_PER_STEP // d)
    nhb = ATT_HEADS // hpb
    band = (1, d, blk, hpb * ATT_HEAD_DIM)
    prev = lambda n: jnp.maximum(n - 1, 0)
    return pl.pallas_call(
        functools.partial(_att_kernel, d=d, hpb=hpb),
        out_shape=(jax.ShapeDtypeStruct((ATT_HEADS, batch * seq, ATT_HEAD_DIM), F32),
                   jax.ShapeDtypeStruct((nhb, batch * seq, LANES), F32)),
        grid=(batch, nb, nhb),
        in_specs=[
            pl.BlockSpec(band, lambda b, n, hb: (b, 0, n, hb)),
            pl.BlockSpec(band, lambda b, n, hb: (b, 0, prev(n), hb)),
            pl.BlockSpec(band, lambda b, n, hb: (b, 0, n, hb)),
            pl.BlockSpec(band, lambda b, n, hb: (b, 0, prev(n), nhb + hb)),
            pl.BlockSpec(band, lambda b, n, hb: (b, 0, n, nhb + hb)),
            pl.BlockSpec((1, hpb, blk, 2 * blk), lambda b, n, hb: (gi, hb, 0, 0)),
            pl.BlockSpec((2, blk, 2 * blk), lambda b, n, hb: (0, 0, 0)),
        ],
        out_specs=(pl.BlockSpec((hpb, d * blk, ATT_HEAD_DIM), lambda b, n, hb: (hb, b * nb + n, 0)),
                   pl.BlockSpec((1, d * blk, LANES), lambda b, n, hb: (hb, b * nb + n, 0))),
        compiler_params=_params(("parallel", "parallel", "parallel")),
        name="dilated_attention_g%d" % gi,
    )(q, kv, kv, kv, kv, bias, jnp.asarray(_band_caps()))


def _merge_out_kernel(o0_ref, o1_ref, o2_ref, l0_ref, l1_ref, l2_ref, x_ref, w_ref, out_ref, m_ref):
    l0 = jnp.sum(l0_ref[...], axis=0)
    l1 = jnp.sum(l1_ref[...], axis=0)
    l2 = jnp.sum(l2_ref[...], axis=0)
    mx = jnp.maximum(jnp.maximum(l0, l1), l2)
    e0 = jnp.exp(l0 - mx)
    e1 = jnp.exp(l1 - mx)
    e2 = jnp.exp(l2 - mx)
    den = e0 + e1 + e2
    w0 = e0 / den
    w1 = e1 / den
    w2 = e2 / den
    tm = l0.shape[0]
    for h in range(ATT_HEADS):
        hs = slice(h * ATT_HEAD_DIM, (h + 1) * ATT_HEAD_DIM)
        b0 = jnp.broadcast_to(w0[:, h:h + 1], (tm, ATT_HEAD_DIM))
        b1 = jnp.broadcast_to(w1[:, h:h + 1], (tm, ATT_HEAD_DIM))
        b2 = jnp.broadcast_to(w2[:, h:h + 1], (tm, ATT_HEAD_DIM))
        m_ref[:, hs] = (b0 * o0_ref[h] + b1 * o1_ref[h] + b2 * o2_ref[h]).astype(m_ref.dtype)
    out_ref[...] = x_ref[...] + jnp.dot(m_ref[...], w_ref[...], preferred_element_type=F32)


def _merge_out(outs, lses, x, w, *, tm):
    t, d = x.shape
    heads = pl.BlockSpec((ATT_HEADS, tm, ATT_HEAD_DIM), lambda i: (0, i, 0))
    narrow = [pl.BlockSpec((l.shape[0], tm, LANES), lambda i: (0, i, 0)) for l in lses]
    return pl.pallas_call(
        _merge_out_kernel,
        out_shape=jax.ShapeDtypeStruct((t, d), F32),
        grid=(t // tm,),
        in_specs=[heads, heads, heads, *narrow,
                  pl.BlockSpec((tm, d), lambda i: (i, 0)),
                  pl.BlockSpec((ATT_WIDTH, d), lambda i: (0, 0))],
        out_specs=pl.BlockSpec((tm, d), lambda i: (i, 0)),
        scratch_shapes=[pltpu.VMEM((tm, ATT_WIDTH), BF16)],
        compiler_params=_params(("parallel",)),
        name="merge_out",
    )(*outs, *lses, x, w)


def _as_residue_major(a, batch, seq):
    return a.reshape(batch, 1, seq, a.shape[1]) if a.ndim == 2 else a


def kernel(x, g_mix, g_ffn, w_ret_in, w_ret_out, g_kv, w_kv, w_att_q, w_att_out, rel_bias, w_ffn_in, w_ffn_out, g_final):
    batch, seq, d = x.shape
    t = batch * seq
    x = x.reshape(t, d)
    tn = 1024
    ret_dims = dict(batch=batch, seq=seq, tm=1024, tn=tn)
    att_dims = dict(batch=batch, seq=seq, tm=512, tn=tn)

    inv = (1.0 / ROPE_BASE ** np.linspace(0.0, 1.0, ROPE_HALF)).astype(np.float32)
    ang = jnp.arange(seq, dtype=F32)[:, None] * jnp.asarray(inv)[None, :]
    rope = (jnp.cos(ang), jnp.sin(ang))

    qt, vt, wt = RET_NQ // tn, RET_NV // tn, ATT_WIDTH // tn
    ret_outs = (
        _Out(2 * RET_NQ, 1, True, (_Seg(0, qt, 0, 1.0), _Seg(qt, qt, qt, RET_QK_DIM ** -0.5))),
        _Out(2 * RET_NV, 1, False, (_Seg(2 * qt, 2 * vt, 0, 1.0),)),
    )
    kv_outs = tuple(
        _Out(2 * ATT_WIDTH, dd, False, (_Seg(gi * wt, wt, 0, 1.0), _Seg((N_GROUPS + gi) * wt, wt, wt, 1.0)))
        for gi, (_, dd) in enumerate(DIL_CONFIGS))
    q_outs = tuple(
        _Out(ATT_WIDTH, dd, False, (_Seg(gi * wt, wt, 0, 1.0),)) for gi, (_, dd) in enumerate(DIL_CONFIGS))

    kvs = bias = None
    for l in range(DEPTH):
        if l < N_RET_LAYERS:
            qk, vg = _norm_matmul(x, g_mix[l], w_ret_in[l].astype(BF16), ret_outs, rope=rope, **ret_dims)
            y = _retention(qk, vg, batch, seq, hps=2)
            x = _matmul_residual(y, w_ret_out[l].astype(BF16), x, tm=512, tn=1024)
        else:
            j = l - N_RET_LAYERS
            if j == 0:
                kvs = _norm_matmul(x, g_kv, w_kv.astype(BF16), kv_outs, **att_dims)
                bias = _rel_bias_tiles(rel_bias)
            qs = _norm_matmul(x, g_mix[l], w_att_q[j].astype(BF16), q_outs, **att_dims)
            outs, lses = [], []
            for gi in range(N_GROUPS):
                o, lse = _dilated_group(_as_residue_major(qs[gi], batch, seq),
                                        _as_residue_major(kvs[gi], batch, seq), bias, gi, batch, seq)
                outs.append(o)
                lses.append(lse)
            x = _merge_out(outs, lses, x, w_att_out[j].astype(BF16), tm=256)
        x = _ffn(x, g_ffn[l], w_ffn_in[l].astype(BF16), w_ffn_out[l].astype(BF16), tm=512, tf=512,
                 final_gain=g_final if l == DEPTH - 1 else None)
    return x.reshape(batch, seq, d)
```

```python
import functools
from typing import NamedTuple

import numpy as np
import jax
import jax.numpy as jnp
from jax import lax
from jax.experimental import pallas as pl
from jax.experimental.pallas import tpu as pltpu

F32 = jnp.float32
BF16 = jnp.bfloat16

D_MODEL = 2048
DEPTH = 4
N_RET_LAYERS = DEPTH // 2

RET_HEADS = 8
RET_QK_DIM = D_MODEL // RET_HEADS
RET_V_DIM = 2 * RET_QK_DIM
RET_NQ = RET_HEADS * RET_QK_DIM
RET_NV = RET_HEADS * RET_V_DIM
ROPE_BASE = 10000.0
ROPE_HALF = RET_QK_DIM // 2

DIL_CONFIGS = ((128, 1), (512, 4), (2048, 16))
N_GROUPS = len(DIL_CONFIGS)
ATT_HEAD_DIM = 128
ATT_HEADS = D_MODEL // ATT_HEAD_DIM
ATT_WIDTH = ATT_HEADS * ATT_HEAD_DIM
ATT_BLK = 128
ATT_TILES_PER_STEP = 64
REL_BUCKETS = 32
REL_MAX_DIST = 2048
FFN_DIM = -(-8 * D_MODEL // (3 * 256)) * 256
NORM_EPS = 1e-6
NEG_INF = -1e30

LANES = 128
RET_CHUNK = 256
RMS_ROWS = 64
VMEM_LIMIT = 56 * 1024 * 1024


def _params(semantics):
    return pltpu.CompilerParams(dimension_semantics=semantics, vmem_limit_bytes=VMEM_LIMIT)


def _rms_rows(x_ref, g_ref, dst_ref, copy_ref=None, stage_ref=None):
    g = g_ref[...]

    def body(c, carry):
        r0 = pl.multiple_of(c * RMS_ROWS, RMS_ROWS)
        x = x_ref[pl.ds(r0, RMS_ROWS), :]
        ms = jnp.mean(x * x, axis=-1, keepdims=True)
        y = x * lax.rsqrt(ms + NORM_EPS) * g
        dst_ref[pl.ds(r0, RMS_ROWS), :] = y.astype(dst_ref.dtype)
        if copy_ref is not None:
            copy_ref[pl.ds(r0, RMS_ROWS), :] = x
        if stage_ref is not None:
            for k in range(stage_ref.shape[0]):
                stage_ref[k, pl.ds(r0, RMS_ROWS), :] = y[:, k * LANES:(k + 1) * LANES]
        return carry

    lax.fori_loop(0, x_ref.shape[0] // RMS_ROWS, body, 0)


class _Seg(NamedTuple):
    j0: int
    nt: int
    cb0: int
    scale: float


class _Out(NamedTuple):
    ncols: int
    dilation: int
    epilogue: str
    segs: tuple


def _col_block(out, j):
    cb = out.segs[0].cb0
    for seg in out.segs:
        cb = jnp.where(j >= seg.j0, seg.cb0 + jnp.minimum(j - seg.j0, seg.nt - 1), cb)
    return cb


def _norm_matmul_kernel(*refs, outs, has_rope, dilations):
    x_ref, g_ref, w_ref = refs[:3]
    pos = 3
    if has_rope:
        cos_ref, sin_ref = refs[3:5]
        pos = 5
    out_refs = refs[pos:pos + len(outs)]
    pos += len(outs)
    h_refs = {1: refs[pos]}
    for k, d in enumerate(dilations):
        h_refs[d] = refs[pos + 1 + k]
    stage_ref = refs[pos + 1 + len(dilations)] if dilations else None
    j = pl.program_id(1)
    tm = x_ref.shape[0]

    @pl.when(j == 0)
    def _():
        _rms_rows(x_ref, g_ref, h_refs[1], stage_ref=stage_ref)
        for d in dilations:
            n_r = tm // d
            for r in range(d):
                for k in range(stage_ref.shape[0]):
                    h_refs[d][r * n_r:(r + 1) * n_r, k * LANES:(k + 1) * LANES] = (
                        stage_ref[k, pl.ds(r, n_r, stride=d), :].astype(BF16))

    def emit(out, o_ref):
        acc = jnp.dot(h_refs[out.dilation][...], w_ref[...], preferred_element_type=F32)
        tn = acc.shape[1]
        if out.dilation > 1:
            n_r = tm // out.dilation
            for r in range(out.dilation):
                o_ref[0, r] = acc[r * n_r:(r + 1) * n_r, :].astype(o_ref.dtype)
        elif out.epilogue == "rope":
            scale = F32(out.segs[0].scale)
            for seg in out.segs[1:]:
                scale = jnp.where(j >= seg.j0, F32(seg.scale), scale)
            cos = cos_ref[...]
            sin = sin_ref[...]
            for c in range(0, tn, 2 * ROPE_HALF):
                t1 = acc[:, c:c + ROPE_HALF] * scale
                t2 = acc[:, c + ROPE_HALF:c + 2 * ROPE_HALF] * scale
                o_ref[:, c:c + ROPE_HALF] = (t1 * cos - t2 * sin).astype(o_ref.dtype)
                o_ref[:, c + ROPE_HALF:c + 2 * ROPE_HALF] = (t2 * cos + t1 * sin).astype(o_ref.dtype)
        elif out.epilogue == "silu":
            o_ref[...] = (acc * (1.0 / (1.0 + jnp.exp(-acc)))).astype(o_ref.dtype)
        else:
            o_ref[...] = acc.astype(o_ref.dtype)

    for out, o_ref in zip(outs, out_refs):
        hit = None
        for seg in out.segs:
            inside = (j >= seg.j0) & (j < seg.j0 + seg.nt)
            hit = inside if hit is None else hit | inside
        pl.when(hit)(functools.partial(emit, out, o_ref))


def _norm_matmul(x, gain, w, layer, outs, *, batch, seq, tm, tn, rope=None):
    t, d = x.shape
    n = w.shape[2]
    tiles_per_batch = seq // tm
    dilations = tuple(sorted({o.dilation for o in outs} - {1}))
    in_specs = [
        pl.BlockSpec((tm, d), lambda i, j: (i, 0)),
        pl.BlockSpec((1, d), lambda i, j: (0, 0)),
        pl.BlockSpec((None, d, tn), lambda i, j: (layer, 0, j)),
    ]
    args = [x, gain.reshape(1, d), w]
    if rope is not None:
        in_specs += [pl.BlockSpec((tm, ROPE_HALF), lambda i, j: (i % tiles_per_batch, 0))] * 2
        args += list(rope)
    out_shapes, out_specs = [], []
    for out in outs:
        dd = out.dilation
        if dd == 1:
            out_shapes.append(jax.ShapeDtypeStruct((t, out.ncols), BF16))
            out_specs.append(pl.BlockSpec((tm, tn), lambda i, j, out=out: (i, _col_block(out, j))))
        else:
            out_shapes.append(jax.ShapeDtypeStruct((batch, dd, seq // dd, out.ncols), BF16))
            out_specs.append(pl.BlockSpec(
                (1, dd, tm // dd, tn),
                lambda i, j, out=out: (i // tiles_per_batch, 0, i % tiles_per_batch, _col_block(out, j))))
    scratch = [pltpu.VMEM((tm, d), BF16) for _ in range(1 + len(dilations))]
    if dilations:
        scratch.append(pltpu.VMEM((d // LANES, tm, LANES), F32))
    return pl.pallas_call(
        functools.partial(_norm_matmul_kernel, outs=tuple(outs), has_rope=rope is not None,
                          dilations=dilations),
        out_shape=tuple(out_shapes),
        grid=(t // tm, n // tn),
        in_specs=in_specs,
        out_specs=tuple(out_specs),
        scratch_shapes=scratch,
        compiler_params=_params(("parallel", "arbitrary")),
        name="norm_matmul",
    )(*args)


def _retention_kernel(lg_ref, cd_ref, q_ref, k_ref, v_ref, g_ref, y_ref, r_ref, dm_ref, qd_ref, kd_ref, *, hps):
    hp = pl.program_id(1)
    n = pl.program_id(2)
    c = q_ref.shape[0]
    dk, dv = RET_QK_DIM, RET_V_DIM

    @pl.when(n == 0)
    def _():
        r_ref[...] = jnp.zeros_like(r_ref)
        i = lax.broadcasted_iota(jnp.int32, (c, c), 0)
        jj = lax.broadcasted_iota(jnp.int32, (c, c), 1)
        diff = i - jj
        row = lax.broadcasted_iota(jnp.int32, (c, LANES), 0).astype(F32)
        for hh in range(hps):
            lg = lg_ref[hp * hps + hh]
            dm_ref[hh] = jnp.where(diff >= 0, jnp.exp(lg * jnp.maximum(diff, 0).astype(F32)), 0.0)
            qd_ref[hh] = jnp.exp(lg * (row + 1.0))
            kd_ref[hh] = jnp.exp(lg * (F32(c - 1) - row))

    for hh in range(hps):
        q = q_ref[:, hh * dk:(hh + 1) * dk]
        k = k_ref[:, hh * dk:(hh + 1) * dk]
        v = v_ref[:, hh * dv:(hh + 1) * dv]
        s = lax.dot_general(q, k, (((1,), (1,)), ((), ())), preferred_element_type=F32) * dm_ref[hh]
        inner = jnp.dot(s.astype(BF16), v, preferred_element_type=F32)
        r_old = r_ref[hh]
        cross = jnp.dot(q, r_old.astype(BF16), preferred_element_type=F32)
        qd = qd_ref[hh]
        kd = kd_ref[hh]
        k_dec = jnp.concatenate(
            [(k[:, o:o + LANES].astype(F32) * kd).astype(BF16) for o in range(0, dk, LANES)], axis=1)
        r_ref[hh] = r_old * cd_ref[hp * hps + hh] + lax.dot_general(
            k_dec, v, (((0,), (0,)), ((), ())), preferred_element_type=F32)

        y = jnp.concatenate(
            [inner[:, o:o + LANES] + cross[:, o:o + LANES] * qd for o in range(0, dv, LANES)], axis=1)
        ms = jnp.mean(y * y, axis=-1, keepdims=True)
        gate = g_ref[:, hh * dv:(hh + 1) * dv].astype(F32)
        y_ref[:, hh * dv:(hh + 1) * dv] = ((y * lax.rsqrt(ms + NORM_EPS)) * gate).astype(y_ref.dtype)


def _retention(qk, v, gate, batch, seq, *, hps):
    t = qk.shape[0]
    c = RET_CHUNK
    nc = seq // c
    nhp = RET_HEADS // hps
    log_g = np.log(1.0 - 2.0 ** (-5.0 - np.arange(RET_HEADS))).astype(np.float32)
    c_dec = np.exp(log_g * c).astype(np.float32)
    smem = pl.BlockSpec(memory_space=pltpu.SMEM)
    qk_shape = (c, hps * RET_QK_DIM)
    v_shape = (c, hps * RET_V_DIM)
    return pl.pallas_call(
        functools.partial(_retention_kernel, hps=hps),
        out_shape=jax.ShapeDtypeStruct((t, RET_NV), BF16),
        grid=(batch, nhp, nc),
        in_specs=[
            smem, smem,
            pl.BlockSpec(qk_shape, lambda b, h, n: (b * nc + n, h)),
            pl.BlockSpec(qk_shape, lambda b, h, n: (b * nc + n, nhp + h)),
            pl.BlockSpec(v_shape, lambda b, h, n: (b * nc + n, h)),
            pl.BlockSpec(v_shape, lambda b, h, n: (b * nc + n, h)),
        ],
        out_specs=pl.BlockSpec(v_shape, lambda b, h, n: (b * nc + n, h)),
        scratch_shapes=[
            pltpu.VMEM((hps, RET_QK_DIM, RET_V_DIM), F32),
            pltpu.VMEM((hps, c, c), F32),
            pltpu.VMEM((hps, c, LANES), F32),
            pltpu.VMEM((hps, c, LANES), F32),
        ],
        compiler_params=_params(("parallel", "parallel", "arbitrary")),
        name="retention",
    )(jnp.asarray(log_g), jnp.asarray(c_dec), qk, qk, v, gate)


def _matmul_residual_kernel(y_ref, w_ref, x_ref, o_ref):
    o_ref[...] = x_ref[...] + jnp.dot(y_ref[...], w_ref[...], preferred_element_type=F32)


def _matmul_residual(y, w, layer, x, *, tm, tn):
    t, k = y.shape
    n = w.shape[2]
    return pl.pallas_call(
        _matmul_residual_kernel,
        out_shape=jax.ShapeDtypeStruct((t, n), F32),
        grid=(n // tn, t // tm),
        in_specs=[
            pl.BlockSpec((tm, k), lambda j, i: (i, 0)),
            pl.BlockSpec((None, k, tn), lambda j, i: (layer, 0, j)),
            pl.BlockSpec((tm, tn), lambda j, i: (i, j)),
        ],
        out_specs=pl.BlockSpec((tm, tn), lambda j, i: (i, j)),
        compiler_params=_params(("parallel", "parallel")),
        name="matmul_residual",
    )(y, w, x)


def _ffn_kernel(*refs, final_norm):
    if final_norm:
        x_ref, g_ref, w1_ref, w2_ref, wo_ref, gf_ref, o_ref, h_ref = refs
    else:
        x_ref, g_ref, w1_ref, w2_ref, wo_ref, o_ref, h_ref = refs
    f = pl.program_id(1)

    @pl.when(f == 0)
    def _():
        _rms_rows(x_ref, g_ref, h_ref, copy_ref=o_ref)

    h = h_ref[...]
    z1 = jnp.dot(h, w1_ref[...], preferred_element_type=F32)
    z2 = jnp.dot(h, w2_ref[...], preferred_element_type=F32)
    a = (z1 * (1.0 / (1.0 + jnp.exp(-z1)))) * z2
    o_ref[...] += jnp.dot(a.astype(BF16), wo_ref[...], preferred_element_type=F32)

    if final_norm:
        @pl.when(f == pl.num_programs(1) - 1)
        def _():
            gf = gf_ref[...]

            def body(c, carry):
                r0 = pl.multiple_of(c * RMS_ROWS, RMS_ROWS)
                x = o_ref[pl.ds(r0, RMS_ROWS), :]
                ms = jnp.mean(x * x, axis=-1, keepdims=True)
                o_ref[pl.ds(r0, RMS_ROWS), :] = x * lax.rsqrt(ms + NORM_EPS) * gf
                return carry

            lax.fori_loop(0, o_ref.shape[0] // RMS_ROWS, body, 0)


def _ffn(x, gain, w_in, w_out, layer, *, tm, tf, final_gain=None):
    t, d = x.shape
    nf = FFN_DIM // tf
    in_specs = [
        pl.BlockSpec((tm, d), lambda i, f: (i, 0), pipeline_mode=pl.Buffered(1)),
        pl.BlockSpec((1, d), lambda i, f: (0, 0)),
        pl.BlockSpec((None, d, tf), lambda i, f: (layer, 0, f)),
        pl.BlockSpec((None, d, tf), lambda i, f: (layer, 0, nf + f)),
        pl.BlockSpec((None, tf, d), lambda i, f: (layer, f, 0)),
    ]
    args = [x, gain.reshape(1, d), w_in, w_in, w_out]
    if final_gain is not None:
        in_specs.append(pl.BlockSpec((1, d), lambda i, f: (0, 0)))
        args.append(final_gain.reshape(1, d))
    return pl.pallas_call(
        functools.partial(_ffn_kernel, final_norm=final_gain is not None),
        out_shape=jax.ShapeDtypeStruct((t, d), F32),
        grid=(t // tm, nf),
        in_specs=in_specs,
        out_specs=pl.BlockSpec((tm, d), lambda i, f: (i, 0)),
        scratch_shapes=[pltpu.VMEM((tm, d), BF16)],
        compiler_params=_params(("parallel", "arbitrary")),
        name="ffn",
    )(*args)


def _t5_bucket_table(dist):
    n = np.maximum(dist, 0)
    max_exact = REL_BUCKETS // 2
    large = max_exact + (np.log(np.maximum(n, 1) / max_exact) / np.log(REL_MAX_DIST / max_exact)
                         * (REL_BUCKETS - max_exact)).astype(np.int32)
    large = np.minimum(large, REL_BUCKETS - 1)
    return np.where(n < max_exact, n, large).astype(np.int32)


def _bias_kernel(tab_ref, bucket_ref, o_ref):
    col = pl.program_id(0) * ATT_HEADS + pl.program_id(1)
    bucket = bucket_ref[0]
    acc = jnp.zeros(bucket.shape, F32)
    for b in range(REL_BUCKETS):
        acc = jnp.where(bucket == b, tab_ref[b, col], acc)
    o_ref[0, 0] = acc


def _rel_bias_tiles(rel_bias):
    blk = ATT_BLK
    i = np.arange(blk)[:, None]
    c = np.arange(2 * blk)[None, :]
    delta = blk + i - c
    buckets = np.stack([_t5_bucket_table(np.maximum(delta, 0) * d) for _, d in DIL_CONFIGS])
    return pl.pallas_call(
        _bias_kernel,
        out_shape=jax.ShapeDtypeStruct((N_GROUPS, ATT_HEADS, blk, 2 * blk), F32),
        grid=(N_GROUPS, ATT_HEADS),
        in_specs=[
            pl.BlockSpec(memory_space=pltpu.SMEM),
            pl.BlockSpec((1, blk, 2 * blk), lambda g, h: (g, 0, 0)),
        ],
        out_specs=pl.BlockSpec((1, 1, blk, 2 * blk), lambda g, h: (g, h, 0, 0)),
        compiler_params=_params(("parallel", "parallel")),
        name="rel_bias",
    )(rel_bias, jnp.asarray(buckets))


def _att_kernel(q_ref, kp_ref, kc_ref, vp_ref, vc_ref, bias_ref, cap_ref, o_ref, lse_ref, *, d, hpb):
    n = pl.program_id(1)
    hb = pl.program_id(2)
    blk = ATT_BLK
    lane = lax.broadcasted_iota(jnp.int32, (blk, LANES), 1)
    scale = ATT_HEAD_DIM ** -0.5
    first = jnp.where(n > 0, 0, 1)

    def residue(r):
        rows = pl.ds(r, blk, stride=d) if d > 1 else slice(None)
        lse_tile = jnp.zeros((blk, LANES), F32)
        for hl in range(hpb):
            hs = slice(hl * ATT_HEAD_DIM, (hl + 1) * ATT_HEAD_DIM)
            q = q_ref[0, r, :, hs]
            kb = jnp.concatenate([kp_ref[0, r, :, hs], kc_ref[0, r, :, hs]], axis=0)
            vb = jnp.concatenate([vp_ref[0, r, :, hs], vc_ref[0, r, :, hs]], axis=0)
            s = lax.dot_general(q, kb, (((1,), (1,)), ((), ())), preferred_element_type=F32)
            s = jnp.minimum(s * scale + bias_ref[0, hl], cap_ref[first])
            m = jnp.max(s, axis=-1, keepdims=True)
            p = jnp.exp(s - m)
            den = jnp.sum(p, axis=-1, keepdims=True)
            o_ref[hl, rows, :] = jnp.dot(p.astype(BF16), vb, preferred_element_type=F32) / den
            lse_tile = jnp.where(lane == hb * hpb + hl, m + jnp.log(den), lse_tile)
        lse_ref[0, rows, :] = lse_tile

    rpi = min(d, max(1, ATT_HEADS // hpb))

    def body(it, carry):
        for rr in range(rpi):
            residue(it * rpi + rr)
        return carry

    if d == rpi:
        body(0, 0)
    else:
        lax.fori_loop(0, d // rpi, body, 0)


def _band_caps():
    blk = ATT_BLK
    i = np.arange(blk)[:, None]
    c = np.arange(2 * blk)[None, :]
    delta = blk + i - c
    band = (delta >= 0) & (delta <= blk)
    caps = np.stack([band, band & (c >= blk)])
    return np.where(caps, np.finfo(np.float32).max, NEG_INF).astype(np.float32)


def _dilated_group(q, kv, bias, gi, batch, seq):
    d = DIL_CONFIGS[gi][1]
    blk = ATT_BLK
    nb = seq // d // blk
    hpb = min(ATT_HEADS, ATT_TILES_PER_STEP // d)
    nhb = ATT_HEADS // hpb
    band = (1, d, blk, hpb * ATT_HEAD_DIM)
    prev = lambda n: jnp.maximum(n - 1, 0)
    return pl.pallas_call(
        functools.partial(_att_kernel, d=d, hpb=hpb),
        out_shape=(jax.ShapeDtypeStruct((ATT_HEADS, batch * seq, ATT_HEAD_DIM), F32),
                   jax.ShapeDtypeStruct((nhb, batch * seq, LANES), F32)),
        grid=(batch, nb, nhb),
        in_specs=[
            pl.BlockSpec(band, lambda b, n, hb: (b, 0, n, hb)),
            pl.BlockSpec(band, lambda b, n, hb: (b, 0, prev(n), hb)),
            pl.BlockSpec(band, lambda b, n, hb: (b, 0, n, hb)),
            pl.BlockSpec(band, lambda b, n, hb: (b, 0, prev(n), nhb + hb)),
            pl.BlockSpec(band, lambda b, n, hb: (b, 0, n, nhb + hb)),
            pl.BlockSpec((1, hpb, blk, 2 * blk), lambda b, n, hb: (gi, hb, 0, 0)),
            pl.BlockSpec((2, blk, 2 * blk), lambda b, n, hb: (0, 0, 0)),
        ],
        out_specs=(pl.BlockSpec((hpb, d * blk, ATT_HEAD_DIM), lambda b, n, hb: (hb, b * nb + n, 0)),
                   pl.BlockSpec((1, d * blk, LANES), lambda b, n, hb: (hb, b * nb + n, 0))),
        compiler_params=_params(("parallel", "parallel", "parallel")),
        name="dilated_attention_g%d" % gi,
    )(q, kv, kv, kv, kv, bias, jnp.asarray(_band_caps()))


def _merge_out_kernel(o0_ref, o1_ref, o2_ref, l0_ref, l1_ref, l2_ref, x_ref, w_ref, out_ref, ma_ref, mb_ref):
    s = pl.program_id(0)

    def step(src_ref, dst_ref):
        out_ref[...] = x_ref[...] + jnp.dot(src_ref[...], w_ref[...], preferred_element_type=F32)
        l0 = jnp.sum(l0_ref[...], axis=0)
        l1 = jnp.sum(l1_ref[...], axis=0)
        l2 = jnp.sum(l2_ref[...], axis=0)
        mx = jnp.maximum(jnp.maximum(l0, l1), l2)
        e0 = jnp.exp(l0 - mx)
        e1 = jnp.exp(l1 - mx)
        e2 = jnp.exp(l2 - mx)
        den = e0 + e1 + e2
        w0 = e0 / den
        w1 = e1 / den
        w2 = e2 / den
        tm = l0.shape[0]
        for h in range(ATT_HEADS):
            hs = slice(h * ATT_HEAD_DIM, (h + 1) * ATT_HEAD_DIM)
            b0 = jnp.broadcast_to(w0[:, h:h + 1], (tm, ATT_HEAD_DIM))
            b1 = jnp.broadcast_to(w1[:, h:h + 1], (tm, ATT_HEAD_DIM))
            b2 = jnp.broadcast_to(w2[:, h:h + 1], (tm, ATT_HEAD_DIM))
            dst_ref[:, hs] = (b0 * o0_ref[h] + b1 * o1_ref[h] + b2 * o2_ref[h]).astype(dst_ref.dtype)

    @pl.when(s == 0)
    def _():
        mb_ref[...] = jnp.zeros_like(mb_ref)

    @pl.when(s % 2 == 0)
    def _():
        step(mb_ref, ma_ref)

    @pl.when(s % 2 == 1)
    def _():
        step(ma_ref, mb_ref)


def _merge_out(outs, lses, x, w, layer, *, tm):
    t, d = x.shape
    nt = t // tm
    cur = lambda s: jnp.minimum(s, nt - 1)
    prv = lambda s: jnp.maximum(s - 1, 0)
    heads = pl.BlockSpec((ATT_HEADS, tm, ATT_HEAD_DIM), lambda s: (0, cur(s), 0))
    narrow = [pl.BlockSpec((l.shape[0], tm, LANES), lambda s: (0, cur(s), 0)) for l in lses]
    return pl.pallas_call(
        _merge_out_kernel,
        out_shape=jax.ShapeDtypeStruct((t, d), F32),
        grid=(nt + 1,),
        in_specs=[heads, heads, heads, *narrow,
                  pl.BlockSpec((tm, d), lambda s: (prv(s), 0)),
                  pl.BlockSpec((None, ATT_WIDTH, d), lambda s: (layer, 0, 0))],
        out_specs=pl.BlockSpec((tm, d), lambda s: (prv(s), 0)),
        scratch_shapes=[pltpu.VMEM((tm, ATT_WIDTH), BF16), pltpu.VMEM((tm, ATT_WIDTH), BF16)],
        compiler_params=_params(("arbitrary",)),
        name="merge_out",
    )(*outs, *lses, x, w)


def _as_residue_major(a, batch, seq):
    return a.reshape(batch, 1, seq, a.shape[1]) if a.ndim == 2 else a


def kernel(x, g_mix, g_ffn, w_ret_in, w_ret_out, g_kv, w_kv, w_att_q, w_att_out, rel_bias, w_ffn_in, w_ffn_out, g_final):
    batch, seq, d = x.shape
    t = batch * seq
    x = x.reshape(t, d)
    tn = 1024
    ret_dims = dict(batch=batch, seq=seq, tm=1024, tn=tn)
    att_dims = dict(batch=batch, seq=seq, tm=512, tn=tn)

    inv = (1.0 / ROPE_BASE ** np.linspace(0.0, 1.0, ROPE_HALF)).astype(np.float32)
    ang = jnp.arange(seq, dtype=F32)[:, None] * jnp.asarray(inv)[None, :]
    rope = (jnp.cos(ang), jnp.sin(ang))

    qt, vt, wt = RET_NQ // tn, RET_NV // tn, ATT_WIDTH // tn
    ret_outs = (
        _Out(2 * RET_NQ, 1, "rope", (_Seg(0, qt, 0, 1.0), _Seg(qt, qt, qt, RET_QK_DIM ** -0.5))),
        _Out(RET_NV, 1, "cast", (_Seg(2 * qt, vt, 0, 1.0),)),
        _Out(RET_NV, 1, "silu", (_Seg(2 * qt + vt, vt, 0, 1.0),)),
    )
    kv_outs = tuple(
        _Out(2 * ATT_WIDTH, dd, "cast", (_Seg(gi * wt, wt, 0, 1.0), _Seg((N_GROUPS + gi) * wt, wt, wt, 1.0)))
        for gi, (_, dd) in enumerate(DIL_CONFIGS))
    q_outs = tuple(
        _Out(ATT_WIDTH, dd, "cast", (_Seg(gi * wt, wt, 0, 1.0),)) for gi, (_, dd) in enumerate(DIL_CONFIGS))

    w_ret_in, w_ret_out, w_att_q, w_att_out, w_ffn_in, w_ffn_out = (
        w.astype(BF16) for w in (w_ret_in, w_ret_out, w_att_q, w_att_out, w_ffn_in, w_ffn_out))
    w_kv = w_kv.astype(BF16)[None]

    kvs = bias = None
    for l in range(DEPTH):
        if l < N_RET_LAYERS:
            qk, v, gate = _norm_matmul(x, g_mix[l], w_ret_in, l, ret_outs, rope=rope, **ret_dims)
            y = _retention(qk, v, gate, batch, seq, hps=2)
            x = _matmul_residual(y, w_ret_out, l, x, tm=512, tn=1024)
        else:
            j = l - N_RET_LAYERS
            if j == 0:
                kvs = _norm_matmul(x, g_kv, w_kv, 0, kv_outs, **att_dims)
                bias = _rel_bias_tiles(rel_bias)
            qs = _norm_matmul(x, g_mix[l], w_att_q, j, q_outs, **att_dims)
            outs, lses = [], []
            for gi in range(N_GROUPS):
                o, lse = _dilated_group(_as_residue_major(qs[gi], batch, seq),
                                        _as_residue_major(kvs[gi], batch, seq), bias, gi, batch, seq)
                outs.append(o)
                lses.append(lse)
            x = _merge_out(outs, lses, x, w_att_out, j, tm=256)
        x = _ffn(x, g_ffn[l], w_ffn_in, w_ffn_out, l, tm=1024, tf=512,
                 final_gain=g_final if l == DEPTH - 1 else None)
    return x.reshape(batch, seq, d)
```

```python
import functools
from typing import NamedTuple

import numpy as np
import jax
import jax.numpy as jnp
from jax import lax
from jax.experimental import pallas as pl
from jax.experimental.pallas import tpu as pltpu

F32 = jnp.float32
BF16 = jnp.bfloat16

D_MODEL = 2048
DEPTH = 4
N_RET_LAYERS = DEPTH // 2

RET_HEADS = 8
RET_QK_DIM = D_MODEL // RET_HEADS
RET_V_DIM = 2 * RET_QK_DIM
RET_NQ = RET_HEADS * RET_QK_DIM
RET_NV = RET_HEADS * RET_V_DIM
ROPE_BASE = 10000.0
ROPE_HALF = RET_QK_DIM // 2

DIL_CONFIGS = ((128, 1), (512, 4), (2048, 16))
N_GROUPS = len(DIL_CONFIGS)
ATT_HEAD_DIM = 128
ATT_HEADS = D_MODEL // ATT_HEAD_DIM
ATT_WIDTH = ATT_HEADS * ATT_HEAD_DIM
ATT_BLK = 128
ATT_TILES_PER_STEP = 64
REL_BUCKETS = 32
REL_MAX_DIST = 2048
FFN_DIM = -(-8 * D_MODEL // (3 * 256)) * 256
NORM_EPS = 1e-6
NEG_INF = -1e30

LANES = 128
RET_CHUNK = 256
RMS_ROWS = 128
VMEM_LIMIT = 56 * 1024 * 1024


def _params(semantics):
    return pltpu.CompilerParams(dimension_semantics=semantics, vmem_limit_bytes=VMEM_LIMIT)


def _rms_rows(x_ref, g_ref, dst_ref, copy_ref=None, stage_ref=None):
    g = g_ref[...]

    def body(c, carry):
        r0 = pl.multiple_of(c * RMS_ROWS, RMS_ROWS)
        x = x_ref[pl.ds(r0, RMS_ROWS), :]
        ms = jnp.mean(x * x, axis=-1, keepdims=True)
        y = x * lax.rsqrt(ms + NORM_EPS) * g
        dst_ref[pl.ds(r0, RMS_ROWS), :] = y.astype(dst_ref.dtype)
        if copy_ref is not None:
            copy_ref[pl.ds(r0, RMS_ROWS), :] = x
        if stage_ref is not None:
            for k in range(stage_ref.shape[0]):
                stage_ref[k, pl.ds(r0, RMS_ROWS), :] = y[:, k * LANES:(k + 1) * LANES]
        return carry

    lax.fori_loop(0, x_ref.shape[0] // RMS_ROWS, body, 0)


class _Seg(NamedTuple):
    j0: int
    nt: int
    cb0: int
    scale: float


class _Out(NamedTuple):
    ncols: int
    dilation: int
    epilogue: str
    segs: tuple


def _col_block(out, j):
    cb = out.segs[0].cb0
    for seg in out.segs:
        cb = jnp.where(j >= seg.j0, seg.cb0 + jnp.minimum(j - seg.j0, seg.nt - 1), cb)
    return cb


def _norm_matmul_kernel(*refs, outs, has_rope, dilations):
    x_ref, g_ref, w_ref = refs[:3]
    pos = 3
    if has_rope:
        cos_ref, sin_ref = refs[3:5]
        pos = 5
    out_refs = refs[pos:pos + len(outs)]
    pos += len(outs)
    h_refs = {1: refs[pos]}
    for k, d in enumerate(dilations):
        h_refs[d] = refs[pos + 1 + k]
    stage_ref = refs[pos + 1 + len(dilations)] if dilations else None
    j = pl.program_id(1)
    tm = x_ref.shape[0]

    @pl.when(j == 0)
    def _():
        _rms_rows(x_ref, g_ref, h_refs[1], stage_ref=stage_ref)
        for d in dilations:
            n_r = tm // d
            for r in range(d):
                for k in range(stage_ref.shape[0]):
                    h_refs[d][r * n_r:(r + 1) * n_r, k * LANES:(k + 1) * LANES] = (
                        stage_ref[k, pl.ds(r, n_r, stride=d), :].astype(BF16))

    def emit(out, o_ref):
        acc = jnp.dot(h_refs[out.dilation][...], w_ref[...], preferred_element_type=F32)
        tn = acc.shape[1]
        if out.dilation > 1:
            n_r = tm // out.dilation
            for r in range(out.dilation):
                o_ref[0, r] = acc[r * n_r:(r + 1) * n_r, :].astype(o_ref.dtype)
        elif out.epilogue == "rope":
            scale = F32(out.segs[0].scale)
            for seg in out.segs[1:]:
                scale = jnp.where(j >= seg.j0, F32(seg.scale), scale)
            cos = cos_ref[...]
            sin = sin_ref[...]
            for c in range(0, tn, 2 * ROPE_HALF):
                t1 = acc[:, c:c + ROPE_HALF] * scale
                t2 = acc[:, c + ROPE_HALF:c + 2 * ROPE_HALF] * scale
                o_ref[:, c:c + ROPE_HALF] = (t1 * cos - t2 * sin).astype(o_ref.dtype)
                o_ref[:, c + ROPE_HALF:c + 2 * ROPE_HALF] = (t2 * cos + t1 * sin).astype(o_ref.dtype)
        elif out.epilogue == "silu":
            o_ref[...] = (acc * (1.0 / (1.0 + jnp.exp(-acc)))).astype(o_ref.dtype)
        else:
            o_ref[...] = acc.astype(o_ref.dtype)

    for out, o_ref in zip(outs, out_refs):
        hit = None
        for seg in out.segs:
            inside = (j >= seg.j0) & (j < seg.j0 + seg.nt)
            hit = inside if hit is None else hit | inside
        pl.when(hit)(functools.partial(emit, out, o_ref))


def _norm_matmul(x, gain, w, layer, outs, *, batch, seq, tm, tn, rope=None):
    t, d = x.shape
    n = w.shape[2]
    tiles_per_batch = seq // tm
    dilations = tuple(sorted({o.dilation for o in outs} - {1}))
    in_specs = [
        pl.BlockSpec((tm, d), lambda i, j: (i, 0)),
        pl.BlockSpec((1, d), lambda i, j: (0, 0)),
        pl.BlockSpec((None, d, tn), lambda i, j: (layer, 0, j)),
    ]
    args = [x, gain.reshape(1, d), w]
    if rope is not None:
        in_specs += [pl.BlockSpec((tm, ROPE_HALF), lambda i, j: (i % tiles_per_batch, 0))] * 2
        args += list(rope)
    out_shapes, out_specs = [], []
    for out in outs:
        dd = out.dilation
        if dd == 1:
            out_shapes.append(jax.ShapeDtypeStruct((t, out.ncols), BF16))
            out_specs.append(pl.BlockSpec((tm, tn), lambda i, j, out=out: (i, _col_block(out, j))))
        else:
            out_shapes.append(jax.ShapeDtypeStruct((batch, dd, seq // dd, out.ncols), BF16))
            out_specs.append(pl.BlockSpec(
                (1, dd, tm // dd, tn),
                lambda i, j, out=out: (i // tiles_per_batch, 0, i % tiles_per_batch, _col_block(out, j))))
    scratch = [pltpu.VMEM((tm, d), BF16) for _ in range(1 + len(dilations))]
    if dilations:
        scratch.append(pltpu.VMEM((d // LANES, tm, LANES), F32))
    return pl.pallas_call(
        functools.partial(_norm_matmul_kernel, outs=tuple(outs), has_rope=rope is not None,
                          dilations=dilations),
        out_shape=tuple(out_shapes),
        grid=(t // tm, n // tn),
        in_specs=in_specs,
        out_specs=tuple(out_specs),
        scratch_shapes=scratch,
        compiler_params=_params(("parallel", "arbitrary")),
        name="norm_matmul",
    )(*args)


def _retention_kernel(lg_ref, cd_ref, q_ref, k_ref, v_ref, g_ref, y_ref, r_ref, dm_ref, qd_ref, kd_ref, *, hps):
    hp = pl.program_id(1)
    n = pl.program_id(2)
    c = q_ref.shape[0]
    dk, dv = RET_QK_DIM, RET_V_DIM

    @pl.when(n == 0)
    def _():
        r_ref[...] = jnp.zeros_like(r_ref)
        i = lax.broadcasted_iota(jnp.int32, (c, c), 0)
        jj = lax.broadcasted_iota(jnp.int32, (c, c), 1)
        diff = i - jj
        row = lax.broadcasted_iota(jnp.int32, (c, LANES), 0).astype(F32)
        for hh in range(hps):
            lg = lg_ref[hp * hps + hh]
            dm_ref[hh] = jnp.where(diff >= 0, jnp.exp(lg * jnp.maximum(diff, 0).astype(F32)), 0.0)
            qd_ref[hh] = jnp.exp(lg * (row + 1.0))
            kd_ref[hh] = jnp.exp(lg * (F32(c - 1) - row))

    for hh in range(hps):
        q = q_ref[:, hh * dk:(hh + 1) * dk]
        k = k_ref[:, hh * dk:(hh + 1) * dk]
        v = v_ref[:, hh * dv:(hh + 1) * dv]
        s = lax.dot_general(q, k, (((1,), (1,)), ((), ())), preferred_element_type=F32) * dm_ref[hh]
        inner = jnp.dot(s.astype(BF16), v, preferred_element_type=F32)
        r_old = r_ref[hh]
        cross = jnp.dot(q, r_old.astype(BF16), preferred_element_type=F32)
        qd = qd_ref[hh]
        kd = kd_ref[hh]
        k_dec = jnp.concatenate(
            [(k[:, o:o + LANES].astype(F32) * kd).astype(BF16) for o in range(0, dk, LANES)], axis=1)
        r_ref[hh] = r_old * cd_ref[hp * hps + hh] + lax.dot_general(
            k_dec, v, (((0,), (0,)), ((), ())), preferred_element_type=F32)

        y = jnp.concatenate(
            [inner[:, o:o + LANES] + cross[:, o:o + LANES] * qd for o in range(0, dv, LANES)], axis=1)
        ms = jnp.mean(y * y, axis=-1, keepdims=True)
        gate = g_ref[:, hh * dv:(hh + 1) * dv].astype(F32)
        y_ref[:, hh * dv:(hh + 1) * dv] = ((y * lax.rsqrt(ms + NORM_EPS)) * gate).astype(y_ref.dtype)


def _retention(qk, v, gate, batch, seq, *, hps):
    t = qk.shape[0]
    c = RET_CHUNK
    nc = seq // c
    nhp = RET_HEADS // hps
    log_g = np.log(1.0 - 2.0 ** (-5.0 - np.arange(RET_HEADS))).astype(np.float32)
    c_dec = np.exp(log_g * c).astype(np.float32)
    smem = pl.BlockSpec(memory_space=pltpu.SMEM)
    qk_shape = (c, hps * RET_QK_DIM)
    v_shape = (c, hps * RET_V_DIM)
    return pl.pallas_call(
        functools.partial(_retention_kernel, hps=hps),
        out_shape=jax.ShapeDtypeStruct((t, RET_NV), BF16),
        grid=(batch, nhp, nc),
        in_specs=[
            smem, smem,
            pl.BlockSpec(qk_shape, lambda b, h, n: (b * nc + n, h)),
            pl.BlockSpec(qk_shape, lambda b, h, n: (b * nc + n, nhp + h)),
            pl.BlockSpec(v_shape, lambda b, h, n: (b * nc + n, h)),
            pl.BlockSpec(v_shape, lambda b, h, n: (b * nc + n, h)),
        ],
        out_specs=pl.BlockSpec(v_shape, lambda b, h, n: (b * nc + n, h)),
        scratch_shapes=[
            pltpu.VMEM((hps, RET_QK_DIM, RET_V_DIM), F32),
            pltpu.VMEM((hps, c, c), F32),
            pltpu.VMEM((hps, c, LANES), F32),
            pltpu.VMEM((hps, c, LANES), F32),
        ],
        compiler_params=_params(("parallel", "parallel", "arbitrary")),
        name="retention",
    )(jnp.asarray(log_g), jnp.asarray(c_dec), qk, qk, v, gate)


def _matmul_residual_kernel(y_ref, w_ref, x_ref, o_ref):
    o_ref[...] = x_ref[...] + jnp.dot(y_ref[...], w_ref[...], preferred_element_type=F32)


def _matmul_residual(y, w, layer, x, *, tm, tn):
    t, k = y.shape
    n = w.shape[2]
    return pl.pallas_call(
        _matmul_residual_kernel,
        out_shape=jax.ShapeDtypeStruct((t, n), F32),
        grid=(n // tn, t // tm),
        in_specs=[
            pl.BlockSpec((tm, k), lambda j, i: (i, 0)),
            pl.BlockSpec((None, k, tn), lambda j, i: (layer, 0, j)),
            pl.BlockSpec((tm, tn), lambda j, i: (i, j)),
        ],
        out_specs=pl.BlockSpec((tm, tn), lambda j, i: (i, j)),
        compiler_params=_params(("parallel", "parallel")),
        name="matmul_residual",
    )(y, w, x)


def _ffn_kernel(*refs, final_norm):
    if final_norm:
        x_ref, g_ref, w1_ref, w2_ref, wo_ref, gf_ref, o_ref, h_ref = refs
    else:
        x_ref, g_ref, w1_ref, w2_ref, wo_ref, o_ref, h_ref = refs
    f = pl.program_id(1)

    @pl.when(f == 0)
    def _():
        _rms_rows(x_ref, g_ref, h_ref, copy_ref=o_ref)

    h = h_ref[...]
    z1 = jnp.dot(h, w1_ref[...], preferred_element_type=F32)
    z2 = jnp.dot(h, w2_ref[...], preferred_element_type=F32)
    a = (z1 * (1.0 / (1.0 + jnp.exp(-z1)))) * z2
    o_ref[...] += jnp.dot(a.astype(BF16), wo_ref[...], preferred_element_type=F32)

    if final_norm:
        @pl.when(f == pl.num_programs(1) - 1)
        def _():
            gf = gf_ref[...]

            def body(c, carry):
                r0 = pl.multiple_of(c * RMS_ROWS, RMS_ROWS)
                x = o_ref[pl.ds(r0, RMS_ROWS), :]
                ms = jnp.mean(x * x, axis=-1, keepdims=True)
                o_ref[pl.ds(r0, RMS_ROWS), :] = x * lax.rsqrt(ms + NORM_EPS) * gf
                return carry

            lax.fori_loop(0, o_ref.shape[0] // RMS_ROWS, body, 0)


def _ffn(x, gain, w_in, w_out, layer, *, tm, tf, final_gain=None):
    t, d = x.shape
    nf = FFN_DIM // tf
    in_specs = [
        pl.BlockSpec((tm, d), lambda i, f: (i, 0)),
        pl.BlockSpec((1, d), lambda i, f: (0, 0)),
        pl.BlockSpec((None, d, tf), lambda i, f: (layer, 0, f)),
        pl.BlockSpec((None, d, tf), lambda i, f: (layer, 0, nf + f)),
        pl.BlockSpec((None, tf, d), lambda i, f: (layer, f, 0)),
    ]
    args = [x, gain.reshape(1, d), w_in, w_in, w_out]
    if final_gain is not None:
        in_specs.append(pl.BlockSpec((1, d), lambda i, f: (0, 0)))
        args.append(final_gain.reshape(1, d))
    return pl.pallas_call(
        functools.partial(_ffn_kernel, final_norm=final_gain is not None),
        out_shape=jax.ShapeDtypeStruct((t, d), F32),
        grid=(t // tm, nf),
        in_specs=in_specs,
        out_specs=pl.BlockSpec((tm, d), lambda i, f: (i, 0)),
        scratch_shapes=[pltpu.VMEM((tm, d), BF16)],
        compiler_params=_params(("parallel", "arbitrary")),
        name="ffn",
    )(*args)


def _t5_bucket_table(dist):
    n = np.maximum(dist, 0)
    max_exact = REL_BUCKETS // 2
    large = max_exact + (np.log(np.maximum(n, 1) / max_exact) / np.log(REL_MAX_DIST / max_exact)
                         * (REL_BUCKETS - max_exact)).astype(np.int32)
    large = np.minimum(large, REL_BUCKETS - 1)
    return np.where(n < max_exact, n, large).astype(np.int32)


def _bias_kernel(tab_ref, bucket_ref, o_ref):
    col = pl.program_id(0) * ATT_HEADS + pl.program_id(1)
    bucket = bucket_ref[0]
    acc = jnp.zeros(bucket.shape, F32)
    for b in range(REL_BUCKETS):
        acc = jnp.where(bucket == b, tab_ref[b, col], acc)
    o_ref[0, 0] = acc * float(np.log2(np.e))


def _rel_bias_tiles(rel_bias):
    blk = ATT_BLK
    i = np.arange(blk)[:, None]
    c = np.arange(2 * blk)[None, :]
    delta = blk + i - c
    buckets = np.stack([_t5_bucket_table(np.maximum(delta, 0) * d) for _, d in DIL_CONFIGS])
    return pl.pallas_call(
        _bias_kernel,
        out_shape=jax.ShapeDtypeStruct((N_GROUPS, ATT_HEADS, blk, 2 * blk), F32),
        grid=(N_GROUPS, ATT_HEADS),
        in_specs=[
            pl.BlockSpec(memory_space=pltpu.SMEM),
            pl.BlockSpec((1, blk, 2 * blk), lambda g, h: (g, 0, 0)),
        ],
        out_specs=pl.BlockSpec((1, 1, blk, 2 * blk), lambda g, h: (g, h, 0, 0)),
        compiler_params=_params(("parallel", "parallel")),
        name="rel_bias",
    )(rel_bias, jnp.asarray(buckets))


def _att_kernel(q_ref, kp_ref, kc_ref, vp_ref, vc_ref, bias_ref, cap_ref, o_ref, lse_ref, *, d, hpb):
    n = pl.program_id(1)
    hb = pl.program_id(2)
    blk = ATT_BLK
    lane = lax.broadcasted_iota(jnp.int32, (blk, LANES), 1)
    log2e = float(np.log2(np.e))
    scale = ATT_HEAD_DIM ** -0.5 * log2e
    first = jnp.where(n > 0, 0, 1)

    def residue(r):
        rows = pl.ds(r, blk, stride=d) if d > 1 else slice(None)
        lse_tile = jnp.zeros((blk, LANES), F32)
        for hl in range(hpb):
            hs = slice(hl * ATT_HEAD_DIM, (hl + 1) * ATT_HEAD_DIM)
            q = q_ref[0, r, :, hs]
            kb = jnp.concatenate([kp_ref[0, r, :, hs], kc_ref[0, r, :, hs]], axis=0)
            vb = jnp.concatenate([vp_ref[0, r, :, hs], vc_ref[0, r, :, hs]], axis=0)
            s = lax.dot_general(q, kb, (((1,), (1,)), ((), ())), preferred_element_type=F32)
            s = jnp.minimum(s * scale + bias_ref[0, hl], cap_ref[first])
            m = jnp.max(s, axis=-1, keepdims=True)
            p = jnp.exp2(s - m)
            den = jnp.sum(p, axis=-1, keepdims=True)
            o_ref[hl, rows, :] = jnp.dot(p.astype(BF16), vb, preferred_element_type=F32) / den
            lse_tile = jnp.where(lane == hb * hpb + hl, (m + jnp.log2(den)) * (1.0 / log2e), lse_tile)
        lse_ref[0, rows, :] = lse_tile

    rpi = min(d, max(1, 2 * ATT_HEADS // hpb))

    def body(it, carry):
        for rr in range(rpi):
            residue(it * rpi + rr)
        return carry

    if d == rpi:
        body(0, 0)
    else:
        lax.fori_loop(0, d // rpi, body, 0)


def _band_caps():
    blk = ATT_BLK
    i = np.arange(blk)[:, None]
    c = np.arange(2 * blk)[None, :]
    delta = blk + i - c
    band = (delta >= 0) & (delta <= blk)
    caps = np.stack([band, band & (c >= blk)])
    return np.where(caps, np.finfo(np.float32).max, NEG_INF).astype(np.float32)


def _dilated_group(q, kv, bias, gi, batch, seq):
    d = DIL_CONFIGS[gi][1]
    blk = ATT_BLK
    nb = seq // d // blk
    hpb = min(ATT_HEADS, ATT_TILES_PER_STEP // d)
    nhb = ATT_HEADS // hpb
    band = (1, d, blk, hpb * ATT_HEAD_DIM)
    prev = lambda n: jnp.maximum(n - 1, 0)
    return pl.pallas_call(
        functools.partial(_att_kernel, d=d, hpb=hpb),
        out_shape=(jax.ShapeDtypeStruct((ATT_HEADS, batch * seq, ATT_HEAD_DIM), F32),
                   jax.ShapeDtypeStruct((nhb, batch * seq, LANES), F32)),
        grid=(batch, nb, nhb),
        in_specs=[
            pl.BlockSpec(band, lambda b, n, hb: (b, 0, n, hb)),
            pl.BlockSpec(band, lambda b, n, hb: (b, 0, prev(n), hb)),
            pl.BlockSpec(band, lambda b, n, hb: (b, 0, n, hb)),
            pl.BlockSpec(band, lambda b, n, hb: (b, 0, prev(n), nhb + hb)),
            pl.BlockSpec(band, lambda b, n, hb: (b, 0, n, nhb + hb)),
            pl.BlockSpec((1, hpb, blk, 2 * blk), lambda b, n, hb: (gi, hb, 0, 0)),
            pl.BlockSpec((2, blk, 2 * blk), lambda b, n, hb: (0, 0, 0)),
        ],
        out_specs=(pl.BlockSpec((hpb, d * blk, ATT_HEAD_DIM), lambda b, n, hb: (hb, b * nb + n, 0)),
                   pl.BlockSpec((1, d * blk, LANES), lambda b, n, hb: (hb, b * nb + n, 0))),
        compiler_params=_params(("parallel", "parallel", "parallel")),
        name="dilated_attention_g%d" % gi,
    )(q, kv, kv, kv, kv, bias, jnp.asarray(_band_caps()))


def _merge_out_kernel(o0_ref, o1_ref, o2_ref, l0_ref, l1_ref, l2_ref, x_ref, w_ref, out_ref, ma_ref, mb_ref):
    s = pl.program_id(0)

    def step(src_ref, dst_ref):
        out_ref[...] = x_ref[...] + jnp.dot(src_ref[...], w_ref[...], preferred_element_type=F32)
        l0 = jnp.sum(l0_ref[...], axis=0)
        l1 = jnp.sum(l1_ref[...], axis=0)
        l2 = jnp.sum(l2_ref[...], axis=0)
        mx = jnp.maximum(jnp.maximum(l0, l1), l2)
        e0 = jnp.exp(l0 - mx)
        e1 = jnp.exp(l1 - mx)
        e2 = jnp.exp(l2 - mx)
        den = e0 + e1 + e2
        w0 = e0 / den
        w1 = e1 / den
        w2 = e2 / den
        tm = l0.shape[0]
        for h in range(ATT_HEADS):
            hs = slice(h * ATT_HEAD_DIM, (h + 1) * ATT_HEAD_DIM)
            b0 = jnp.broadcast_to(w0[:, h:h + 1], (tm, ATT_HEAD_DIM))
            b1 = jnp.broadcast_to(w1[:, h:h + 1], (tm, ATT_HEAD_DIM))
            b2 = jnp.broadcast_to(w2[:, h:h + 1], (tm, ATT_HEAD_DIM))
            dst_ref[:, hs] = (b0 * o0_ref[h] + b1 * o1_ref[h] + b2 * o2_ref[h]).astype(dst_ref.dtype)

    @pl.when(s == 0)
    def _():
        mb_ref[...] = jnp.zeros_like(mb_ref)

    @pl.when(s % 2 == 0)
    def _():
        step(mb_ref, ma_ref)

    @pl.when(s % 2 == 1)
    def _():
        step(ma_ref, mb_ref)


def _merge_out(outs, lses, x, w, layer, *, tm):
    t, d = x.shape
    nt = t // tm
    cur = lambda s: jnp.minimum(s, nt - 1)
    prv = lambda s: jnp.maximum(s - 1, 0)
    heads = pl.BlockSpec((ATT_HEADS, tm, ATT_HEAD_DIM), lambda s: (0, cur(s), 0))
    narrow = [pl.BlockSpec((l.shape[0], tm, LANES), lambda s: (0, cur(s), 0)) for l in lses]
    return pl.pallas_call(
        _merge_out_kernel,
        out_shape=jax.ShapeDtypeStruct((t, d), F32),
        grid=(nt + 1,),
        in_specs=[heads, heads, heads, *narrow,
                  pl.BlockSpec((tm, d), lambda s: (prv(s), 0)),
                  pl.BlockSpec((None, ATT_WIDTH, d), lambda s: (layer, 0, 0))],
        out_specs=pl.BlockSpec((tm, d), lambda s: (prv(s), 0)),
        scratch_shapes=[pltpu.VMEM((tm, ATT_WIDTH), BF16), pltpu.VMEM((tm, ATT_WIDTH), BF16)],
        compiler_params=_params(("arbitrary",)),
        name="merge_out",
    )(*outs, *lses, x, w)


def _as_residue_major(a, batch, seq):
    return a.reshape(batch, 1, seq, a.shape[1]) if a.ndim == 2 else a


def kernel(x, g_mix, g_ffn, w_ret_in, w_ret_out, g_kv, w_kv, w_att_q, w_att_out, rel_bias, w_ffn_in, w_ffn_out, g_final):
    batch, seq, d = x.shape
    t = batch * seq
    x = x.reshape(t, d)
    tn = 1024
    ret_dims = dict(batch=batch, seq=seq, tm=1024, tn=tn)
    att_dims = dict(batch=batch, seq=seq, tm=512, tn=tn)

    inv = (1.0 / ROPE_BASE ** np.linspace(0.0, 1.0, ROPE_HALF)).astype(np.float32)
    ang = jnp.arange(seq, dtype=F32)[:, None] * jnp.asarray(inv)[None, :]
    rope = (jnp.cos(ang), jnp.sin(ang))

    qt, vt, wt = RET_NQ // tn, RET_NV // tn, ATT_WIDTH // tn
    ret_outs = (
        _Out(2 * RET_NQ, 1, "rope", (_Seg(0, qt, 0, 1.0), _Seg(qt, qt, qt, RET_QK_DIM ** -0.5))),
        _Out(RET_NV, 1, "cast", (_Seg(2 * qt, vt, 0, 1.0),)),
        _Out(RET_NV, 1, "silu", (_Seg(2 * qt + vt, vt, 0, 1.0),)),
    )
    kv_outs = tuple(
        _Out(2 * ATT_WIDTH, dd, "cast", (_Seg(gi * wt, wt, 0, 1.0), _Seg((N_GROUPS + gi) * wt, wt, wt, 1.0)))
        for gi, (_, dd) in enumerate(DIL_CONFIGS))
    q_outs = tuple(
        _Out(ATT_WIDTH, dd, "cast", (_Seg(gi * wt, wt, 0, 1.0),)) for gi, (_, dd) in enumerate(DIL_CONFIGS))

    w_ret_in, w_ret_out, w_att_q, w_att_out, w_ffn_in, w_ffn_out = (
        w.astype(BF16) for w in (w_ret_in, w_ret_out, w_att_q, w_att_out, w_ffn_in, w_ffn_out))
    w_kv = w_kv.astype(BF16)[None]

    kvs = bias = None
    for l in range(DEPTH):
        if l < N_RET_LAYERS:
            qk, v, gate = _norm_matmul(x, g_mix[l], w_ret_in, l, ret_outs, rope=rope, **ret_dims)
            y = _retention(qk, v, gate, batch, seq, hps=4)
            x = _matmul_residual(y, w_ret_out, l, x, tm=512, tn=1024)
        else:
            j = l - N_RET_LAYERS
            if j == 0:
                kvs = _norm_matmul(x, g_kv, w_kv, 0, kv_outs, **att_dims)
                bias = _rel_bias_tiles(rel_bias)
            qs = _norm_matmul(x, g_mix[l], w_att_q, j, q_outs, **att_dims)
            outs, lses = [], []
            for gi in range(N_GROUPS):
                o, lse = _dilated_group(_as_residue_major(qs[gi], batch, seq),
                                        _as_residue_major(kvs[gi], batch, seq), bias, gi, batch, seq)
                outs.append(o)
                lses.append(lse)
            x = _merge_out(outs, lses, x, w_att_out, j, tm=256)
        x = _ffn(x, g_ffn[l], w_ffn_in, w_ffn_out, l, tm=1024, tf=512,
                 final_gain=g_final if l == DEPTH - 1 else None)
    return x.reshape(batch, seq, d)
```

```python
import functools
from typing import NamedTuple

import numpy as np
import jax
import jax.numpy as jnp
from jax import lax
from jax.experimental import pallas as pl
from jax.experimental.pallas import tpu as pltpu

F32 = jnp.float32
BF16 = jnp.bfloat16

D_MODEL = 2048
DEPTH = 4
N_RET_LAYERS = DEPTH // 2

RET_HEADS = 8
RET_QK_DIM = D_MODEL // RET_HEADS
RET_V_DIM = 2 * RET_QK_DIM
RET_NQ = RET_HEADS * RET_QK_DIM
RET_NV = RET_HEADS * RET_V_DIM
ROPE_BASE = 10000.0
ROPE_HALF = RET_QK_DIM // 2

DIL_CONFIGS = ((128, 1), (512, 4), (2048, 16))
N_GROUPS = len(DIL_CONFIGS)
ATT_HEAD_DIM = 128
ATT_HEADS = D_MODEL // ATT_HEAD_DIM
ATT_WIDTH = ATT_HEADS * ATT_HEAD_DIM
ATT_BLK = 128
ATT_TILES_PER_STEP = 64
REL_BUCKETS = 32
REL_MAX_DIST = 2048
FFN_DIM = -(-8 * D_MODEL // (3 * 256)) * 256
NORM_EPS = 1e-6
NEG_INF = -1e30

LANES = 128
RET_CHUNK = 256
RMS_ROWS = 128
VMEM_PHYSICAL = 64 * 1024 * 1024
VMEM_LIMIT = VMEM_PHYSICAL * 7 // 8


def _params(semantics):
    return pltpu.CompilerParams(dimension_semantics=semantics, vmem_limit_bytes=VMEM_LIMIT)


def _rms_rows(x_ref, g_ref, dst_ref, copy_ref=None, stage_ref=None):
    g = g_ref[...]

    def body(c, carry):
        r0 = pl.multiple_of(c * RMS_ROWS, RMS_ROWS)
        x = x_ref[pl.ds(r0, RMS_ROWS), :]
        ms = jnp.mean(x * x, axis=-1, keepdims=True)
        y = x * lax.rsqrt(ms + NORM_EPS) * g
        dst_ref[pl.ds(r0, RMS_ROWS), :] = y.astype(dst_ref.dtype)
        if copy_ref is not None:
            copy_ref[pl.ds(r0, RMS_ROWS), :] = x
        if stage_ref is not None:
            for k in range(stage_ref.shape[0]):
                stage_ref[k, pl.ds(r0, RMS_ROWS), :] = y[:, k * LANES:(k + 1) * LANES]
        return carry

    lax.fori_loop(0, x_ref.shape[0] // RMS_ROWS, body, 0)


def _cast_specs(casts, n_outer, n_inner, outer, inner):
    in_specs, out_specs, out_shapes = [], [], []
    for w, wl in casts:
        rows, cols = w.shape[1:]
        m = max(k for k in range(1, n_inner + 1) if cols % (k * LANES) == 0)
        blk = (None, rows // n_outer, cols // m)
        in_specs.append(pl.BlockSpec(
            blk, lambda *g, wl=wl, m=m: (wl, outer(*g), jnp.minimum(inner(*g), m - 1))))
        out_specs.append(pl.BlockSpec(
            blk, lambda *g, m=m: (0, outer(*g), jnp.minimum(inner(*g), m - 1))))
        out_shapes.append(jax.ShapeDtypeStruct((1, rows, cols), BF16))
    return in_specs, out_specs, out_shapes


def _run_casts(src_refs, dst_refs):
    for src_ref, dst_ref in zip(src_refs, dst_refs):
        dst_ref[...] = src_ref[...].astype(dst_ref.dtype)


class _Seg(NamedTuple):
    j0: int
    nt: int
    cb0: int
    scale: float


class _Out(NamedTuple):
    ncols: int
    dilation: int
    epilogue: str
    segs: tuple


def _col_block(out, j):
    cb = out.segs[0].cb0
    for seg in out.segs:
        cb = jnp.where(j >= seg.j0, seg.cb0 + jnp.minimum(j - seg.j0, seg.nt - 1), cb)
    return cb


def _norm_matmul_kernel(*refs, outs, has_rope, dilations, n_cast):
    x_ref, g_ref, w_ref = refs[:3]
    pos = 3
    if has_rope:
        cos_ref, sin_ref = refs[3:5]
        pos = 5
    cast_src = refs[pos:pos + n_cast]
    pos += n_cast
    out_refs = refs[pos:pos + len(outs)]
    pos += len(outs)
    cast_dst = refs[pos:pos + n_cast]
    pos += n_cast
    h_refs = {1: refs[pos]}
    for k, d in enumerate(dilations):
        h_refs[d] = refs[pos + 1 + k]
    stage_ref = refs[pos + 1 + len(dilations)] if dilations else None
    j = pl.program_id(1)
    tm = x_ref.shape[0]

    @pl.when(j == 0)
    def _():
        _rms_rows(x_ref, g_ref, h_refs[1], stage_ref=stage_ref)
        for d in dilations:
            n_r = tm // d
            for r in range(d):
                for k in range(stage_ref.shape[0]):
                    h_refs[d][r * n_r:(r + 1) * n_r, k * LANES:(k + 1) * LANES] = (
                        stage_ref[k, pl.ds(r, n_r, stride=d), :].astype(BF16))

    def emit(out, o_ref):
        _run_casts(cast_src, cast_dst)
        acc = jnp.dot(h_refs[out.dilation][...], w_ref[...], preferred_element_type=F32)
        tn = acc.shape[1]
        if out.dilation > 1:
            n_r = tm // out.dilation
            for r in range(out.dilation):
                o_ref[0, r] = acc[r * n_r:(r + 1) * n_r, :].astype(o_ref.dtype)
        elif out.epilogue == "rope":
            scale = F32(out.segs[0].scale)
            for seg in out.segs[1:]:
                scale = jnp.where(j >= seg.j0, F32(seg.scale), scale)
            cos = cos_ref[...]
            sin = sin_ref[...]
            for c in range(0, tn, 2 * ROPE_HALF):
                t1 = acc[:, c:c + ROPE_HALF] * scale
                t2 = acc[:, c + ROPE_HALF:c + 2 * ROPE_HALF] * scale
                o_ref[:, c:c + ROPE_HALF] = (t1 * cos - t2 * sin).astype(o_ref.dtype)
                o_ref[:, c + ROPE_HALF:c + 2 * ROPE_HALF] = (t2 * cos + t1 * sin).astype(o_ref.dtype)
        elif out.epilogue == "silu":
            o_ref[...] = (acc * (1.0 / (1.0 + jnp.exp(-acc)))).astype(o_ref.dtype)
        else:
            o_ref[...] = acc.astype(o_ref.dtype)

    for out, o_ref in zip(outs, out_refs):
        hit = None
        for seg in out.segs:
            inside = (j >= seg.j0) & (j < seg.j0 + seg.nt)
            hit = inside if hit is None else hit | inside
        pl.when(hit)(functools.partial(emit, out, o_ref))


def _norm_matmul(x, gain, w, layer, outs, *, batch, seq, tm, tn, rope=None, single_buffer_x=False, casts=()):
    t, d = x.shape
    n = w.shape[2]
    tiles_per_batch = seq // tm
    dilations = tuple(sorted({o.dilation for o in outs} - {1}))
    x_mode = dict(pipeline_mode=pl.Buffered(1)) if single_buffer_x else {}
    in_specs = [
        pl.BlockSpec((tm, d), lambda i, j: (i, 0), **x_mode),
        pl.BlockSpec((1, d), lambda i, j: (0, 0)),
        pl.BlockSpec((None, d, tn), lambda i, j: (layer, 0, j)),
    ]
    args = [x, gain.reshape(1, d), w]
    if rope is not None:
        in_specs += [pl.BlockSpec((tm, ROPE_HALF), lambda i, j: (i % tiles_per_batch, 0))] * 2
        args += list(rope)
    cast_in, cast_out, cast_shapes = _cast_specs(casts, t // tm, n // tn, lambda i, j: i, lambda i, j: j)
    in_specs += cast_in
    args += [w_ for w_, _ in casts]
    out_shapes, out_specs = [], []
    for out in outs:
        dd = out.dilation
        if dd == 1:
            out_shapes.append(jax.ShapeDtypeStruct((t, out.ncols), BF16))
            out_specs.append(pl.BlockSpec((tm, tn), lambda i, j, out=out: (i, _col_block(out, j))))
        else:
            out_shapes.append(jax.ShapeDtypeStruct((batch, dd, seq // dd, out.ncols), BF16))
            out_specs.append(pl.BlockSpec(
                (1, dd, tm // dd, tn),
                lambda i, j, out=out: (i // tiles_per_batch, 0, i % tiles_per_batch, _col_block(out, j))))
    scratch = [pltpu.VMEM((tm, d), BF16) for _ in range(1 + len(dilations))]
    if dilations:
        scratch.append(pltpu.VMEM((d // LANES, tm, LANES), F32))
    res = pl.pallas_call(
        functools.partial(_norm_matmul_kernel, outs=tuple(outs), has_rope=rope is not None,
                          dilations=dilations, n_cast=len(casts)),
        out_shape=tuple(out_shapes + cast_shapes),
        grid=(t // tm, n // tn),
        in_specs=in_specs,
        out_specs=tuple(out_specs + cast_out),
        scratch_shapes=scratch,
        compiler_params=_params(("parallel", "arbitrary")),
        name="norm_matmul",
    )(*args)
    return res[:len(outs)], res[len(outs):]


def _retention_kernel(lg_ref, cd_ref, q_ref, k_ref, v_ref, g_ref, y_ref, r_ref, dm_ref, qd_ref, kd_ref, *, hps):
    hp = pl.program_id(1)
    n = pl.program_id(2)
    c = q_ref.shape[0]
    dk, dv = RET_QK_DIM, RET_V_DIM

    @pl.when(n == 0)
    def _():
        r_ref[...] = jnp.zeros_like(r_ref)
        i = lax.broadcasted_iota(jnp.int32, (c, c), 0)
        jj = lax.broadcasted_iota(jnp.int32, (c, c), 1)
        diff = i - jj
        row = lax.broadcasted_iota(jnp.int32, (c, LANES), 0).astype(F32)
        for hh in range(hps):
            lg = lg_ref[hp * hps + hh]
            dm_ref[hh] = jnp.where(diff >= 0, jnp.exp(lg * jnp.maximum(diff, 0).astype(F32)), 0.0)
            qd_ref[hh] = jnp.exp(lg * (row + 1.0))
            kd_ref[hh] = jnp.exp(lg * (F32(c - 1) - row))

    for hh in range(hps):
        q = q_ref[:, hh * dk:(hh + 1) * dk]
        k = k_ref[:, hh * dk:(hh + 1) * dk]
        v = v_ref[:, hh * dv:(hh + 1) * dv]
        s = lax.dot_general(q, k, (((1,), (1,)), ((), ())), preferred_element_type=F32) * dm_ref[hh]
        inner = jnp.dot(s.astype(BF16), v, preferred_element_type=F32)
        r_old = r_ref[hh]
        cross = jnp.dot(q, r_old.astype(BF16), preferred_element_type=F32)
        qd = qd_ref[hh]
        kd = kd_ref[hh]
        k_dec = jnp.concatenate(
            [(k[:, o:o + LANES].astype(F32) * kd).astype(BF16) for o in range(0, dk, LANES)], axis=1)
        r_ref[hh] = r_old * cd_ref[hp * hps + hh] + lax.dot_general(
            k_dec, v, (((0,), (0,)), ((), ())), preferred_element_type=F32)

        y = jnp.concatenate(
            [inner[:, o:o + LANES] + cross[:, o:o + LANES] * qd for o in range(0, dv, LANES)], axis=1)
        ms = jnp.mean(y * y, axis=-1, keepdims=True)
        gate = g_ref[:, hh * dv:(hh + 1) * dv].astype(F32)
        y_ref[:, hh * dv:(hh + 1) * dv] = ((y * lax.rsqrt(ms + NORM_EPS)) * gate).astype(y_ref.dtype)


def _retention(qk, v, gate, batch, seq, *, hps):
    t = qk.shape[0]
    c = RET_CHUNK
    nc = seq // c
    nhp = RET_HEADS // hps
    log_g = np.log(1.0 - 2.0 ** (-5.0 - np.arange(RET_HEADS))).astype(np.float32)
    c_dec = np.exp(log_g * c).astype(np.float32)
    smem = pl.BlockSpec(memory_space=pltpu.SMEM)
    qk_shape = (c, hps * RET_QK_DIM)
    v_shape = (c, hps * RET_V_DIM)
    return pl.pallas_call(
        functools.partial(_retention_kernel, hps=hps),
        out_shape=jax.ShapeDtypeStruct((t, RET_NV), BF16),
        grid=(batch, nhp, nc),
        in_specs=[
            smem, smem,
            pl.BlockSpec(qk_shape, lambda b, h, n: (b * nc + n, h)),
            pl.BlockSpec(qk_shape, lambda b, h, n: (b * nc + n, nhp + h)),
            pl.BlockSpec(v_shape, lambda b, h, n: (b * nc + n, h)),
            pl.BlockSpec(v_shape, lambda b, h, n: (b * nc + n, h)),
        ],
        out_specs=pl.BlockSpec(v_shape, lambda b, h, n: (b * nc + n, h)),
        scratch_shapes=[
            pltpu.VMEM((hps, RET_QK_DIM, RET_V_DIM), F32),
            pltpu.VMEM((hps, c, c), F32),
            pltpu.VMEM((hps, c, LANES), F32),
            pltpu.VMEM((hps, c, LANES), F32),
        ],
        compiler_params=_params(("parallel", "parallel", "arbitrary")),
        name="retention",
    )(jnp.asarray(log_g), jnp.asarray(c_dec), qk, qk, v, gate)


def _matmul_residual_kernel(*refs, n_cast):
    y_ref, w_ref, x_ref = refs[:3]
    cast_src = refs[3:3 + n_cast]
    o_ref = refs[3 + n_cast]
    cast_dst = refs[4 + n_cast:]
    o_ref[...] = x_ref[...] + jnp.dot(y_ref[...], w_ref[...], preferred_element_type=F32)
    _run_casts(cast_src, cast_dst)


def _matmul_residual(y, w, layer, x, *, tm, tn, casts=()):
    t, k = y.shape
    n = w.shape[2]
    cast_in, cast_out, cast_shapes = _cast_specs(casts, t // tm, n // tn, lambda j, i: i, lambda j, i: j)
    res = pl.pallas_call(
        functools.partial(_matmul_residual_kernel, n_cast=len(casts)),
        out_shape=(jax.ShapeDtypeStruct((t, n), F32), *cast_shapes),
        grid=(n // tn, t // tm),
        in_specs=[
            pl.BlockSpec((tm, k), lambda j, i: (i, 0)),
            pl.BlockSpec((None, k, tn), lambda j, i: (layer, 0, j)),
            pl.BlockSpec((tm, tn), lambda j, i: (i, j)),
            *cast_in,
        ],
        out_specs=(pl.BlockSpec((tm, tn), lambda j, i: (i, j)), *cast_out),
        compiler_params=_params(("parallel", "parallel")),
        name="matmul_residual",
    )(y, w, x, *[w_ for w_, _ in casts])
    return res[0], res[1:]


def _ffn_kernel(*refs, final_norm):
    if final_norm:
        x_ref, g_ref, w1_ref, w2_ref, wo_ref, gf_ref, o_ref, h_ref = refs
    else:
        x_ref, g_ref, w1_ref, w2_ref, wo_ref, o_ref, h_ref = refs
    f = pl.program_id(1)

    @pl.when(f == 0)
    def _():
        _rms_rows(x_ref, g_ref, h_ref, copy_ref=o_ref)

    h = h_ref[...]
    z1 = jnp.dot(h, w1_ref[...], preferred_element_type=F32)
    z2 = jnp.dot(h, w2_ref[...], preferred_element_type=F32)
    a = (z1 * (1.0 / (1.0 + jnp.exp(-z1)))) * z2
    o_ref[...] += jnp.dot(a.astype(BF16), wo_ref[...], preferred_element_type=F32)

    if final_norm:
        @pl.when(f == pl.num_programs(1) - 1)
        def _():
            gf = gf_ref[...]

            def body(c, carry):
                r0 = pl.multiple_of(c * RMS_ROWS, RMS_ROWS)
                x = o_ref[pl.ds(r0, RMS_ROWS), :]
                ms = jnp.mean(x * x, axis=-1, keepdims=True)
                o_ref[pl.ds(r0, RMS_ROWS), :] = x * lax.rsqrt(ms + NORM_EPS) * gf
                return carry

            lax.fori_loop(0, o_ref.shape[0] // RMS_ROWS, body, 0)


def _ffn(x, gain, w_in, w_out, layer, *, tm, tf, final_gain=None):
    t, d = x.shape
    nf = FFN_DIM // tf
    in_specs = [
        pl.BlockSpec((tm, d), lambda i, f: (i, 0)),
        pl.BlockSpec((1, d), lambda i, f: (0, 0)),
        pl.BlockSpec((None, d, tf), lambda i, f: (layer, 0, f)),
        pl.BlockSpec((None, d, tf), lambda i, f: (layer, 0, nf + f)),
        pl.BlockSpec((None, tf, d), lambda i, f: (layer, f, 0)),
    ]
    args = [x, gain.reshape(1, d), w_in, w_in, w_out]
    if final_gain is not None:
        in_specs.append(pl.BlockSpec((1, d), lambda i, f: (0, 0)))
        args.append(final_gain.reshape(1, d))
    return pl.pallas_call(
        functools.partial(_ffn_kernel, final_norm=final_gain is not None),
        out_shape=jax.ShapeDtypeStruct((t, d), F32),
        grid=(t // tm, nf),
        in_specs=in_specs,
        out_specs=pl.BlockSpec((tm, d), lambda i, f: (i, 0)),
        scratch_shapes=[pltpu.VMEM((tm, d), BF16)],
        compiler_params=_params(("parallel", "arbitrary")),
        name="ffn",
    )(*args)


def _t5_bucket_table(dist):
    n = np.maximum(dist, 0)
    max_exact = REL_BUCKETS // 2
    large = max_exact + (np.log(np.maximum(n, 1) / max_exact) / np.log(REL_MAX_DIST / max_exact)
                         * (REL_BUCKETS - max_exact)).astype(np.int32)
    large = np.minimum(large, REL_BUCKETS - 1)
    return np.where(n < max_exact, n, large).astype(np.int32)


def _bias_kernel(tab_ref, bucket_ref, o_ref):
    col = pl.program_id(0) * ATT_HEADS + pl.program_id(1)
    bucket = bucket_ref[0]
    acc = jnp.zeros(bucket.shape, F32)
    for b in range(REL_BUCKETS):
        acc = jnp.where(bucket == b, tab_ref[b, col], acc)
    o_ref[0, 0] = acc * float(np.log2(np.e))


def _rel_bias_tiles(rel_bias):
    blk = ATT_BLK
    i = np.arange(blk)[:, None]
    c = np.arange(2 * blk)[None, :]
    delta = blk + i - c
    buckets = np.stack([_t5_bucket_table(np.maximum(delta, 0) * d) for _, d in DIL_CONFIGS])
    return pl.pallas_call(
        _bias_kernel,
        out_shape=jax.ShapeDtypeStruct((N_GROUPS, ATT_HEADS, blk, 2 * blk), F32),
        grid=(N_GROUPS, ATT_HEADS),
        in_specs=[
            pl.BlockSpec(memory_space=pltpu.SMEM),
            pl.BlockSpec((1, blk, 2 * blk), lambda g, h: (g, 0, 0)),
        ],
        out_specs=pl.BlockSpec((1, 1, blk, 2 * blk), lambda g, h: (g, h, 0, 0)),
        compiler_params=_params(("parallel", "parallel")),
        name="rel_bias",
    )(rel_bias, jnp.asarray(buckets))


def _att_kernel(q_ref, kp_ref, kc_ref, vp_ref, vc_ref, bias_ref, cap_ref, o_ref, lse_ref, *, d, hpb):
    n = pl.program_id(1)
    hb = pl.program_id(2)
    blk = ATT_BLK
    lane = lax.broadcasted_iota(jnp.int32, (blk, LANES), 1)
    log2e = float(np.log2(np.e))
    scale = ATT_HEAD_DIM ** -0.5 * log2e
    first = jnp.where(n > 0, 0, 1)

    def residue(r):
        rows = pl.ds(r, blk, stride=d) if d > 1 else slice(None)
        lse_tile = jnp.zeros((blk, LANES), F32)
        for hl in range(hpb):
            hs = slice(hl * ATT_HEAD_DIM, (hl + 1) * ATT_HEAD_DIM)
            q = q_ref[0, r, :, hs]
            kb = jnp.concatenate([kp_ref[0, r, :, hs], kc_ref[0, r, :, hs]], axis=0)
            vb = jnp.concatenate([vp_ref[0, r, :, hs], vc_ref[0, r, :, hs]], axis=0)
            s = lax.dot_general(q, kb, (((1,), (1,)), ((), ())), preferred_element_type=F32)
            s = jnp.minimum(s * scale + bias_ref[0, hl], cap_ref[first])
            m = jnp.max(s, axis=-1, keepdims=True)
            p = jnp.exp2(s - m)
            den = jnp.sum(p, axis=-1, keepdims=True)
            o_ref[hl, rows, :] = jnp.dot(p.astype(BF16), vb, preferred_element_type=F32) / den
            lse_tile = jnp.where(lane == hb * hpb + hl, (m + jnp.log2(den)) * (1.0 / log2e), lse_tile)
        lse_ref[0, rows, :] = lse_tile

    rpi = min(d, max(1, 2 * ATT_HEADS // hpb))

    def body(it, carry):
        for rr in range(rpi):
            residue(it * rpi + rr)
        return carry

    if d == rpi:
        body(0, 0)
    else:
        lax.fori_loop(0, d // rpi, body, 0)


def _band_caps():
    blk = ATT_BLK
    i = np.arange(blk)[:, None]
    c = np.arange(2 * blk)[None, :]
    delta = blk + i - c
    band = (delta >= 0) & (delta <= blk)
    caps = np.stack([band, band & (c >= blk)])
    return np.where(caps, np.finfo(np.float32).max, NEG_INF).astype(np.float32)


def _dilated_group(q, kv, bias, gi, batch, seq):
    d = DIL_CONFIGS[gi][1]
    blk = ATT_BLK
    nb = seq // d // blk
    hpb = min(ATT_HEADS, ATT_TILES_PER_STEP // d)
    nhb = ATT_HEADS // hpb
    band = (1, d, blk, hpb * ATT_HEAD_DIM)
    prev = lambda n: jnp.maximum(n - 1, 0)
    return pl.pallas_call(
        functools.partial(_att_kernel, d=d, hpb=hpb),
        out_shape=(jax.ShapeDtypeStruct((ATT_HEADS, batch * seq, ATT_HEAD_DIM), F32),
                   jax.ShapeDtypeStruct((nhb, batch * seq, LANES), F32)),
        grid=(batch, nb, nhb),
        in_specs=[
            pl.BlockSpec(band, lambda b, n, hb: (b, 0, n, hb)),
            pl.BlockSpec(band, lambda b, n, hb: (b, 0, prev(n), hb)),
            pl.BlockSpec(band, lambda b, n, hb: (b, 0, n, hb)),
            pl.BlockSpec(band, lambda b, n, hb: (b, 0, prev(n), nhb + hb)),
            pl.BlockSpec(band, lambda b, n, hb: (b, 0, n, nhb + hb)),
            pl.BlockSpec((1, hpb, blk, 2 * blk), lambda b, n, hb: (gi, hb, 0, 0)),
            pl.BlockSpec((2, blk, 2 * blk), lambda b, n, hb: (0, 0, 0)),
        ],
        out_specs=(pl.BlockSpec((hpb, d * blk, ATT_HEAD_DIM), lambda b, n, hb: (hb, b * nb + n, 0)),
                   pl.BlockSpec((1, d * blk, LANES), lambda b, n, hb: (hb, b * nb + n, 0))),
        compiler_params=_params(("parallel", "parallel", "parallel")),
        name="dilated_attention_g%d" % gi,
    )(q, kv, kv, kv, kv, bias, jnp.asarray(_band_caps()))


def _merge_out_kernel(o0_ref, o1_ref, o2_ref, l0_ref, l1_ref, l2_ref, x_ref, w_ref, out_ref, ma_ref, mb_ref):
    s = pl.program_id(0)

    def step(src_ref, dst_ref):
        out_ref[...] = x_ref[...] + jnp.dot(src_ref[...], w_ref[...], preferred_element_type=F32)
        l0 = jnp.sum(l0_ref[...], axis=0)
        l1 = jnp.sum(l1_ref[...], axis=0)
        l2 = jnp.sum(l2_ref[...], axis=0)
        mx = jnp.maximum(jnp.maximum(l0, l1), l2)
        e0 = jnp.exp(l0 - mx)
        e1 = jnp.exp(l1 - mx)
        e2 = jnp.exp(l2 - mx)
        den = e0 + e1 + e2
        w0 = e0 / den
        w1 = e1 / den
        w2 = e2 / den
        tm = l0.shape[0]
        for h in range(ATT_HEADS):
            hs = slice(h * ATT_HEAD_DIM, (h + 1) * ATT_HEAD_DIM)
            b0 = jnp.broadcast_to(w0[:, h:h + 1], (tm, ATT_HEAD_DIM))
            b1 = jnp.broadcast_to(w1[:, h:h + 1], (tm, ATT_HEAD_DIM))
            b2 = jnp.broadcast_to(w2[:, h:h + 1], (tm, ATT_HEAD_DIM))
            dst_ref[:, hs] = (b0 * o0_ref[h] + b1 * o1_ref[h] + b2 * o2_ref[h]).astype(dst_ref.dtype)

    @pl.when(s == 0)
    def _():
        mb_ref[...] = jnp.zeros_like(mb_ref)

    @pl.when(s % 2 == 0)
    def _():
        step(mb_ref, ma_ref)

    @pl.when(s % 2 == 1)
    def _():
        step(ma_ref, mb_ref)


def _merge_out(outs, lses, x, w, layer, *, tm):
    t, d = x.shape
    nt = t // tm
    cur = lambda s: jnp.minimum(s, nt - 1)
    prv = lambda s: jnp.maximum(s - 1, 0)
    heads = pl.BlockSpec((ATT_HEADS, tm, ATT_HEAD_DIM), lambda s: (0, cur(s), 0))
    narrow = [pl.BlockSpec((l.shape[0], tm, LANES), lambda s: (0, cur(s), 0)) for l in lses]
    return pl.pallas_call(
        _merge_out_kernel,
        out_shape=jax.ShapeDtypeStruct((t, d), F32),
        grid=(nt + 1,),
        in_specs=[heads, heads, heads, *narrow,
                  pl.BlockSpec((tm, d), lambda s: (prv(s), 0)),
                  pl.BlockSpec((None, ATT_WIDTH, d), lambda s: (layer, 0, 0))],
        out_specs=pl.BlockSpec((tm, d), lambda s: (prv(s), 0)),
        scratch_shapes=[pltpu.VMEM((tm, ATT_WIDTH), BF16), pltpu.VMEM((tm, ATT_WIDTH), BF16)],
        compiler_params=_params(("arbitrary",)),
        name="merge_out",
    )(*outs, *lses, x, w)


def _as_residue_major(a, batch, seq):
    return a.reshape(batch, 1, seq, a.shape[1]) if a.ndim == 2 else a


def kernel(x, g_mix, g_ffn, w_ret_in, w_ret_out, g_kv, w_kv, w_att_q, w_att_out, rel_bias, w_ffn_in, w_ffn_out, g_final):
    batch, seq, d = x.shape
    t = batch * seq
    x = x.reshape(t, d)
    tn = 1024
    ret_dims = dict(batch=batch, seq=seq, tm=1024, tn=tn)
    att_dims = dict(batch=batch, seq=seq, tm=1024, tn=tn, single_buffer_x=True)

    inv = (1.0 / ROPE_BASE ** np.linspace(0.0, 1.0, ROPE_HALF)).astype(np.float32)
    ang = jnp.arange(seq, dtype=F32)[:, None] * jnp.asarray(inv)[None, :]
    rope = (jnp.cos(ang), jnp.sin(ang))

    qt, vt, wt = RET_NQ // tn, RET_NV // tn, ATT_WIDTH // tn
    ret_outs = (
        _Out(2 * RET_NQ, 1, "rope", (_Seg(0, qt, 0, 1.0), _Seg(qt, qt, qt, RET_QK_DIM ** -0.5))),
        _Out(RET_NV, 1, "cast", (_Seg(2 * qt, vt, 0, 1.0),)),
        _Out(RET_NV, 1, "silu", (_Seg(2 * qt + vt, vt, 0, 1.0),)),
    )
    kv_outs = tuple(
        _Out(2 * ATT_WIDTH, dd, "cast", (_Seg(gi * wt, wt, 0, 1.0), _Seg((N_GROUPS + gi) * wt, wt, wt, 1.0)))
        for gi, (_, dd) in enumerate(DIL_CONFIGS))
    q_outs = tuple(
        _Out(ATT_WIDTH, dd, "cast", (_Seg(gi * wt, wt, 0, 1.0),)) for gi, (_, dd) in enumerate(DIL_CONFIGS))

    def whole(w):
        return w.reshape((1, -1) + w.shape[-1:])

    ret_in = [w_ret_in[0].astype(BF16)[None], None]
    ret_out = [w_ret_out[0].astype(BF16)[None], None]
    ffn_in = [None, None, None, w_ffn_in[3].astype(BF16)[None]]
    ffn_out = [None, None, None, w_ffn_out[3].astype(BF16)[None]]
    w_kv_b = att_q = att_out = None

    kvs = bias = None
    for l in range(DEPTH):
        if l == 0:
            (qk, v, gate), (ret_in[1], ret_out[1], ffn_in[0], ffn_out[0]) = _norm_matmul(
                x, g_mix[l], ret_in[l], 0, ret_outs, rope=rope, **ret_dims,
                casts=[(w_ret_in, 1), (w_ret_out, 1), (w_ffn_in, 0), (w_ffn_out, 0)])
            y = _retention(qk, v, gate, batch, seq, hps=4)
            x, (ffn_in[2],) = _matmul_residual(y, ret_out[l], 0, x, tm=512, tn=1024, casts=[(w_ffn_in, 2)])
        elif l == 1:
            (qk, v, gate), (ffn_in[1], ffn_out[1], w_kv_b, att_q) = _norm_matmul(
                x, g_mix[l], ret_in[l], 0, ret_outs, rope=rope, **ret_dims,
                casts=[(w_ffn_in, 1), (w_ffn_out, 1), (whole(w_kv), 0), (whole(w_att_q), 0)])
            att_q = att_q.reshape(w_att_q.shape)
            y = _retention(qk, v, gate, batch, seq, hps=4)
            x, (ffn_out[2], att_out) = _matmul_residual(
                y, ret_out[l], 0, x, tm=512, tn=1024, casts=[(w_ffn_out, 2), (whole(w_att_out), 0)])
            att_out = att_out.reshape(w_att_out.shape)
        else:
            j = l - N_RET_LAYERS
            if j == 0:
                kvs, _ = _norm_matmul(x, g_kv, w_kv_b, 0, kv_outs, **att_dims)
                bias = _rel_bias_tiles(rel_bias)
            qs, _ = _norm_matmul(x, g_mix[l], att_q, j, q_outs, **att_dims)
            outs, lses = [], []
            for gi in range(N_GROUPS):
                o, lse = _dilated_group(_as_residue_major(qs[gi], batch, seq),
                                        _as_residue_major(kvs[gi], batch, seq), bias, gi, batch, seq)
                outs.append(o)
                lses.append(lse)
            x = _merge_out(outs, lses, x, att_out, j, tm=256)
        x = _ffn(x, g_ffn[l], ffn_in[l], ffn_out[l], 0, tm=1024, tf=512,
                 final_gain=g_final if l == DEPTH - 1 else None)
    return x.reshape(batch, seq, d)
```

```python
import functools
from typing import NamedTuple

import numpy as np
import jax
import jax.numpy as jnp
from jax import lax
from jax.experimental import pallas as pl
from jax.experimental.pallas import tpu as pltpu

F32 = jnp.float32
BF16 = jnp.bfloat16

D_MODEL = 2048
DEPTH = 4
N_RET_LAYERS = DEPTH // 2

RET_HEADS = 8
RET_QK_DIM = D_MODEL // RET_HEADS
RET_V_DIM = 2 * RET_QK_DIM
RET_NQ = RET_HEADS * RET_QK_DIM
RET_NV = RET_HEADS * RET_V_DIM
ROPE_BASE = 10000.0
ROPE_HALF = RET_QK_DIM // 2

DIL_CONFIGS = ((128, 1), (512, 4), (2048, 16))
N_GROUPS = len(DIL_CONFIGS)
ATT_HEAD_DIM = 128
ATT_HEADS = D_MODEL // ATT_HEAD_DIM
ATT_WIDTH = ATT_HEADS * ATT_HEAD_DIM
ATT_BLK = 128
ATT_TILES_PER_STEP = 64
REL_BUCKETS = 32
REL_MAX_DIST = 2048
FFN_DIM = -(-8 * D_MODEL // (3 * 256)) * 256
NORM_EPS = 1e-6
NEG_INF = -1e30

LANES = 128
RET_CHUNK = 256
RMS_ROWS = 128
VMEM_PHYSICAL = 64 * 1024 * 1024
VMEM_LIMIT = VMEM_PHYSICAL * 7 // 8


def _params(semantics):
    return pltpu.CompilerParams(dimension_semantics=semantics, vmem_limit_bytes=VMEM_LIMIT)


def _rms_rows(x_ref, g_ref, dst_ref, copy_ref=None, stage_ref=None):
    g = g_ref[...]

    def body(c, carry):
        r0 = pl.multiple_of(c * RMS_ROWS, RMS_ROWS)
        x = x_ref[pl.ds(r0, RMS_ROWS), :]
        ms = jnp.mean(x * x, axis=-1, keepdims=True)
        y = x * lax.rsqrt(ms + NORM_EPS) * g
        dst_ref[pl.ds(r0, RMS_ROWS), :] = y.astype(dst_ref.dtype)
        if copy_ref is not None:
            copy_ref[pl.ds(r0, RMS_ROWS), :] = x
        if stage_ref is not None:
            for k in range(stage_ref.shape[0]):
                stage_ref[k, pl.ds(r0, RMS_ROWS), :] = y[:, k * LANES:(k + 1) * LANES]
        return carry

    lax.fori_loop(0, x_ref.shape[0] // RMS_ROWS, body, 0)


def _cast_specs(casts, n_outer, n_inner, outer, inner):
    in_specs, out_specs, out_shapes = [], [], []
    for w, wl in casts:
        rows, cols = w.shape[1:]
        m = max(k for k in range(1, n_inner + 1) if cols % (k * LANES) == 0)
        blk = (None, rows // n_outer, cols // m)
        in_specs.append(pl.BlockSpec(
            blk, lambda *g, wl=wl, m=m: (wl, outer(*g), jnp.minimum(inner(*g), m - 1))))
        out_specs.append(pl.BlockSpec(
            blk, lambda *g, m=m: (0, outer(*g), jnp.minimum(inner(*g), m - 1))))
        out_shapes.append(jax.ShapeDtypeStruct((1, rows, cols), BF16))
    return in_specs, out_specs, out_shapes


def _run_casts(src_refs, dst_refs):
    for src_ref, dst_ref in zip(src_refs, dst_refs):
        dst_ref[...] = src_ref[...].astype(dst_ref.dtype)


class _Seg(NamedTuple):
    j0: int
    nt: int
    cb0: int
    scale: float


class _Out(NamedTuple):
    ncols: int
    dilation: int
    epilogue: str
    segs: tuple


def _col_block(out, j):
    cb = out.segs[0].cb0
    for seg in out.segs:
        cb = jnp.where(j >= seg.j0, seg.cb0 + jnp.minimum(j - seg.j0, seg.nt - 1), cb)
    return cb


def _norm_matmul_kernel(*refs, outs, has_rope, dilations, n_cast):
    x_ref, g_ref, w_ref = refs[:3]
    pos = 3
    if has_rope:
        cos_ref, sin_ref = refs[3:5]
        pos = 5
    cast_src = refs[pos:pos + n_cast]
    pos += n_cast
    out_refs = refs[pos:pos + len(outs)]
    pos += len(outs)
    cast_dst = refs[pos:pos + n_cast]
    pos += n_cast
    h_refs = {1: refs[pos]}
    for k, d in enumerate(dilations):
        h_refs[d] = refs[pos + 1 + k]
    stage_ref = refs[pos + 1 + len(dilations)] if dilations else None
    j = pl.program_id(1)
    tm = x_ref.shape[0]

    @pl.when(j == 0)
    def _():
        _rms_rows(x_ref, g_ref, h_refs[1], stage_ref=stage_ref)
        for d in dilations:
            n_r = tm // d
            for r in range(d):
                for k in range(stage_ref.shape[0]):
                    h_refs[d][r * n_r:(r + 1) * n_r, k * LANES:(k + 1) * LANES] = (
                        stage_ref[k, pl.ds(r, n_r, stride=d), :].astype(BF16))

    def emit(out, o_ref):
        _run_casts(cast_src, cast_dst)
        acc = jnp.dot(h_refs[out.dilation][...], w_ref[...], preferred_element_type=F32)
        tn = acc.shape[1]
        if out.dilation > 1:
            n_r = tm // out.dilation
            for r in range(out.dilation):
                o_ref[0, r] = acc[r * n_r:(r + 1) * n_r, :].astype(o_ref.dtype)
        elif out.epilogue == "rope":
            scale = F32(out.segs[0].scale)
            for seg in out.segs[1:]:
                scale = jnp.where(j >= seg.j0, F32(seg.scale), scale)
            cos = cos_ref[...]
            sin = sin_ref[...]
            for c in range(0, tn, 2 * ROPE_HALF):
                t1 = acc[:, c:c + ROPE_HALF] * scale
                t2 = acc[:, c + ROPE_HALF:c + 2 * ROPE_HALF] * scale
                o_ref[:, c:c + ROPE_HALF] = (t1 * cos - t2 * sin).astype(o_ref.dtype)
                o_ref[:, c + ROPE_HALF:c + 2 * ROPE_HALF] = (t2 * cos + t1 * sin).astype(o_ref.dtype)
        elif out.epilogue == "silu":
            o_ref[...] = (acc * (1.0 / (1.0 + jnp.exp(-acc)))).astype(o_ref.dtype)
        else:
            o_ref[...] = acc.astype(o_ref.dtype)

    for out, o_ref in zip(outs, out_refs):
        hit = None
        for seg in out.segs:
            inside = (j >= seg.j0) & (j < seg.j0 + seg.nt)
            hit = inside if hit is None else hit | inside
        pl.when(hit)(functools.partial(emit, out, o_ref))


def _norm_matmul(x, gain, w, layer, outs, *, batch, seq, tm, tn, rope=None, single_buffer_x=False, casts=()):
    t, d = x.shape
    n = w.shape[2]
    tiles_per_batch = seq // tm
    dilations = tuple(sorted({o.dilation for o in outs} - {1}))
    x_mode = dict(pipeline_mode=pl.Buffered(1)) if single_buffer_x else {}
    in_specs = [
        pl.BlockSpec((tm, d), lambda i, j: (i, 0), **x_mode),
        pl.BlockSpec((1, d), lambda i, j: (0, 0)),
        pl.BlockSpec((None, d, tn), lambda i, j: (layer, 0, j)),
    ]
    args = [x, gain.reshape(1, d), w]
    if rope is not None:
        in_specs += [pl.BlockSpec((tm, ROPE_HALF), lambda i, j: (i % tiles_per_batch, 0))] * 2
        args += list(rope)
    cast_in, cast_out, cast_shapes = _cast_specs(casts, t // tm, n // tn, lambda i, j: i, lambda i, j: j)
    in_specs += cast_in
    args += [w_ for w_, _ in casts]
    out_shapes, out_specs = [], []
    for out in outs:
        dd = out.dilation
        if dd == 1:
            out_shapes.append(jax.ShapeDtypeStruct((t, out.ncols), BF16))
            out_specs.append(pl.BlockSpec((tm, tn), lambda i, j, out=out: (i, _col_block(out, j))))
        else:
            out_shapes.append(jax.ShapeDtypeStruct((batch, dd, seq // dd, out.ncols), BF16))
            out_specs.append(pl.BlockSpec(
                (1, dd, tm // dd, tn),
                lambda i, j, out=out: (i // tiles_per_batch, 0, i % tiles_per_batch, _col_block(out, j))))
    scratch = [pltpu.VMEM((tm, d), BF16) for _ in range(1 + len(dilations))]
    if dilations:
        scratch.append(pltpu.VMEM((d // LANES, tm, LANES), F32))
    res = pl.pallas_call(
        functools.partial(_norm_matmul_kernel, outs=tuple(outs), has_rope=rope is not None,
                          dilations=dilations, n_cast=len(casts)),
        out_shape=tuple(out_shapes + cast_shapes),
        grid=(t // tm, n // tn),
        in_specs=in_specs,
        out_specs=tuple(out_specs + cast_out),
        scratch_shapes=scratch,
        compiler_params=_params(("parallel", "arbitrary")),
        name="norm_matmul",
    )(*args)
    return res[:len(outs)], res[len(outs):]


def _retention_kernel(lg_ref, cd_ref, q_ref, k_ref, v_ref, g_ref, y_ref, r_ref, dm_ref, qd_ref, kd_ref, *, hps):
    hp = pl.program_id(1)
    n = pl.program_id(2)
    c = q_ref.shape[0]
    dk, dv = RET_QK_DIM, RET_V_DIM

    @pl.when(n == 0)
    def _():
        r_ref[...] = jnp.zeros_like(r_ref)
        i = lax.broadcasted_iota(jnp.int32, (c, c), 0)
        jj = lax.broadcasted_iota(jnp.int32, (c, c), 1)
        diff = i - jj
        row = lax.broadcasted_iota(jnp.int32, (c, LANES), 0).astype(F32)
        for hh in range(hps):
            lg = lg_ref[hp * hps + hh]
            dm_ref[hh] = jnp.where(diff >= 0, jnp.exp(lg * jnp.maximum(diff, 0).astype(F32)), 0.0)
            qd_ref[hh] = jnp.exp(lg * (row + 1.0))
            kd_ref[hh] = jnp.exp(lg * (F32(c - 1) - row))

    for hh in range(hps):
        q = q_ref[:, hh * dk:(hh + 1) * dk]
        k = k_ref[:, hh * dk:(hh + 1) * dk]
        v = v_ref[:, hh * dv:(hh + 1) * dv]
        s = lax.dot_general(q, k, (((1,), (1,)), ((), ())), preferred_element_type=F32) * dm_ref[hh]
        inner = jnp.dot(s.astype(BF16), v, preferred_element_type=F32)
        r_old = r_ref[hh]
        cross = jnp.dot(q, r_old.astype(BF16), preferred_element_type=F32)
        qd = qd_ref[hh]
        kd = kd_ref[hh]
        k_dec = jnp.concatenate(
            [(k[:, o:o + LANES].astype(F32) * kd).astype(BF16) for o in range(0, dk, LANES)], axis=1)
        r_ref[hh] = r_old * cd_ref[hp * hps + hh] + lax.dot_general(
            k_dec, v, (((0,), (0,)), ((), ())), preferred_element_type=F32)

        y = jnp.concatenate(
            [inner[:, o:o + LANES] + cross[:, o:o + LANES] * qd for o in range(0, dv, LANES)], axis=1)
        ms = jnp.mean(y * y, axis=-1, keepdims=True)
        gate = g_ref[:, hh * dv:(hh + 1) * dv].astype(F32)
        y_ref[:, hh * dv:(hh + 1) * dv] = ((y * lax.rsqrt(ms + NORM_EPS)) * gate).astype(y_ref.dtype)


def _retention(qk, v, gate, batch, seq, *, hps):
    t = qk.shape[0]
    c = RET_CHUNK
    nc = seq // c
    nhp = RET_HEADS // hps
    log_g = np.log(1.0 - 2.0 ** (-5.0 - np.arange(RET_HEADS))).astype(np.float32)
    c_dec = np.exp(log_g * c).astype(np.float32)
    smem = pl.BlockSpec(memory_space=pltpu.SMEM)
    qk_shape = (c, hps * RET_QK_DIM)
    v_shape = (c, hps * RET_V_DIM)
    return pl.pallas_call(
        functools.partial(_retention_kernel, hps=hps),
        out_shape=jax.ShapeDtypeStruct((t, RET_NV), BF16),
        grid=(batch, nhp, nc),
        in_specs=[
            smem, smem,
            pl.BlockSpec(qk_shape, lambda b, h, n: (b * nc + n, h)),
            pl.BlockSpec(qk_shape, lambda b, h, n: (b * nc + n, nhp + h)),
            pl.BlockSpec(v_shape, lambda b, h, n: (b * nc + n, h)),
            pl.BlockSpec(v_shape, lambda b, h, n: (b * nc + n, h)),
        ],
        out_specs=pl.BlockSpec(v_shape, lambda b, h, n: (b * nc + n, h)),
        scratch_shapes=[
            pltpu.VMEM((hps, RET_QK_DIM, RET_V_DIM), F32),
            pltpu.VMEM((hps, c, c), F32),
            pltpu.VMEM((hps, c, LANES), F32),
            pltpu.VMEM((hps, c, LANES), F32),
        ],
        compiler_params=_params(("parallel", "parallel", "arbitrary")),
        name="retention",
    )(jnp.asarray(log_g), jnp.asarray(c_dec), qk, qk, v, gate)


def _matmul_residual_kernel(*refs, n_cast):
    y_ref, w_ref, x_ref = refs[:3]
    cast_src = refs[3:3 + n_cast]
    o_ref = refs[3 + n_cast]
    cast_dst = refs[4 + n_cast:]
    o_ref[...] = x_ref[...] + jnp.dot(y_ref[...], w_ref[...], preferred_element_type=F32)
    _run_casts(cast_src, cast_dst)


def _matmul_residual(y, w, layer, x, *, tm, tn, casts=()):
    t, k = y.shape
    n = w.shape[2]
    cast_in, cast_out, cast_shapes = _cast_specs(casts, t // tm, n // tn, lambda j, i: i, lambda j, i: j)
    res = pl.pallas_call(
        functools.partial(_matmul_residual_kernel, n_cast=len(casts)),
        out_shape=(jax.ShapeDtypeStruct((t, n), F32), *cast_shapes),
        grid=(n // tn, t // tm),
        in_specs=[
            pl.BlockSpec((tm, k), lambda j, i: (i, 0)),
            pl.BlockSpec((None, k, tn), lambda j, i: (layer, 0, j)),
            pl.BlockSpec((tm, tn), lambda j, i: (i, j)),
            *cast_in,
        ],
        out_specs=(pl.BlockSpec((tm, tn), lambda j, i: (i, j)), *cast_out),
        compiler_params=_params(("parallel", "parallel")),
        name="matmul_residual",
    )(y, w, x, *[w_ for w_, _ in casts])
    return res[0], res[1:]


def _ffn_kernel(*refs, final_norm):
    if final_norm:
        x_ref, g_ref, w1_ref, w2_ref, wo_ref, gf_ref, o_ref, h_ref = refs
    else:
        x_ref, g_ref, w1_ref, w2_ref, wo_ref, o_ref, h_ref = refs
    f = pl.program_id(1)

    @pl.when(f == 0)
    def _():
        _rms_rows(x_ref, g_ref, h_ref, copy_ref=o_ref)

    h = h_ref[...]
    z1 = jnp.dot(h, w1_ref[...], preferred_element_type=F32)
    z2 = jnp.dot(h, w2_ref[...], preferred_element_type=F32)
    a = (z1 * (1.0 / (1.0 + jnp.exp(-z1)))) * z2
    o_ref[...] += jnp.dot(a.astype(BF16), wo_ref[...], preferred_element_type=F32)

    if final_norm:
        @pl.when(f == pl.num_programs(1) - 1)
        def _():
            gf = gf_ref[...]

            def body(c, carry):
                r0 = pl.multiple_of(c * RMS_ROWS, RMS_ROWS)
                x = o_ref[pl.ds(r0, RMS_ROWS), :]
                ms = jnp.mean(x * x, axis=-1, keepdims=True)
                o_ref[pl.ds(r0, RMS_ROWS), :] = x * lax.rsqrt(ms + NORM_EPS) * gf
                return carry

            lax.fori_loop(0, o_ref.shape[0] // RMS_ROWS, body, 0)


def _ffn(x, gain, w_in, w_out, layer, *, tm, tf, final_gain=None):
    t, d = x.shape
    nf = FFN_DIM // tf
    in_specs = [
        pl.BlockSpec((tm, d), lambda i, f: (i, 0)),
        pl.BlockSpec((1, d), lambda i, f: (0, 0)),
        pl.BlockSpec((None, d, tf), lambda i, f: (layer, 0, f)),
        pl.BlockSpec((None, d, tf), lambda i, f: (layer, 0, nf + f)),
        pl.BlockSpec((None, tf, d), lambda i, f: (layer, f, 0)),
    ]
    args = [x, gain.reshape(1, d), w_in, w_in, w_out]
    if final_gain is not None:
        in_specs.append(pl.BlockSpec((1, d), lambda i, f: (0, 0)))
        args.append(final_gain.reshape(1, d))
    return pl.pallas_call(
        functools.partial(_ffn_kernel, final_norm=final_gain is not None),
        out_shape=jax.ShapeDtypeStruct((t, d), F32),
        grid=(t // tm, nf),
        in_specs=in_specs,
        out_specs=pl.BlockSpec((tm, d), lambda i, f: (i, 0)),
        scratch_shapes=[pltpu.VMEM((tm, d), BF16)],
        compiler_params=_params(("parallel", "arbitrary")),
        name="ffn",
    )(*args)


def _t5_bucket_table(dist):
    n = np.maximum(dist, 0)
    max_exact = REL_BUCKETS // 2
    large = max_exact + (np.log(np.maximum(n, 1) / max_exact) / np.log(REL_MAX_DIST / max_exact)
                         * (REL_BUCKETS - max_exact)).astype(np.int32)
    large = np.minimum(large, REL_BUCKETS - 1)
    return np.where(n < max_exact, n, large).astype(np.int32)


def _bias_kernel(tab_ref, bucket_ref, o_ref):
    col = pl.program_id(0) * ATT_HEADS + pl.program_id(1)
    bucket = bucket_ref[0]
    acc = jnp.zeros(bucket.shape, F32)
    for b in range(REL_BUCKETS):
        acc = jnp.where(bucket == b, tab_ref[b, col], acc)
    o_ref[0, 0] = acc * float(np.log2(np.e))


def _rel_bias_tiles(rel_bias):
    blk = ATT_BLK
    i = np.arange(blk)[:, None]
    c = np.arange(2 * blk)[None, :]
    delta = blk + i - c
    buckets = np.stack([_t5_bucket_table(np.maximum(delta, 0) * d) for _, d in DIL_CONFIGS])
    return pl.pallas_call(
        _bias_kernel,
        out_shape=jax.ShapeDtypeStruct((N_GROUPS, ATT_HEADS, blk, 2 * blk), F32),
        grid=(N_GROUPS, ATT_HEADS),
        in_specs=[
            pl.BlockSpec(memory_space=pltpu.SMEM),
            pl.BlockSpec((1, blk, 2 * blk), lambda g, h: (g, 0, 0)),
        ],
        out_specs=pl.BlockSpec((1, 1, blk, 2 * blk), lambda g, h: (g, h, 0, 0)),
        compiler_params=_params(("parallel", "parallel")),
        name="rel_bias",
    )(rel_bias, jnp.asarray(buckets))


def _att_kernel(q_ref, kp_ref, kc_ref, vp_ref, vc_ref, bias_ref, cap_ref, o_ref, lse_ref, *, d, hpb):
    n = pl.program_id(1)
    hb = pl.program_id(2)
    blk = ATT_BLK
    lane = lax.broadcasted_iota(jnp.int32, (blk, LANES), 1)
    log2e = float(np.log2(np.e))
    scale = ATT_HEAD_DIM ** -0.5 * log2e
    first = jnp.where(n > 0, 0, 1)

    def residue(r):
        rows = pl.ds(r, blk, stride=d) if d > 1 else slice(None)
        lse_tile = jnp.zeros((blk, LANES), F32)
        pair = []
        for hl in range(hpb):
            hs = slice(hl * ATT_HEAD_DIM, (hl + 1) * ATT_HEAD_DIM)
            q = q_ref[0, r, :, hs]
            kb = jnp.concatenate([kp_ref[0, r, :, hs], kc_ref[0, r, :, hs]], axis=0)
            vb = jnp.concatenate([vp_ref[0, r, :, hs], vc_ref[0, r, :, hs]], axis=0)
            s = lax.dot_general(q, kb, (((1,), (1,)), ((), ())), preferred_element_type=F32)
            s = jnp.minimum(s * scale + bias_ref[0, hl], cap_ref[first])
            m = jnp.max(s, axis=-1, keepdims=True)
            p = jnp.exp2(s - m)
            den = jnp.sum(p, axis=-1, keepdims=True)
            pair.append(jnp.dot(p.astype(BF16), vb, preferred_element_type=F32) / den)
            if len(pair) == 2:
                o_ref[hl // 2, rows, :] = pltpu.pack_elementwise(pair, packed_dtype=BF16)
                pair = []
            lse_tile = jnp.where(lane == hb * hpb + hl, (m + jnp.log2(den)) * (1.0 / log2e), lse_tile)
        lse_ref[0, rows, :] = lse_tile

    rpi = min(d, max(1, 2 * ATT_HEADS // hpb))

    def body(it, carry):
        for rr in range(rpi):
            residue(it * rpi + rr)
        return carry

    if d == rpi:
        body(0, 0)
    else:
        lax.fori_loop(0, d // rpi, body, 0)


def _band_caps():
    blk = ATT_BLK
    i = np.arange(blk)[:, None]
    c = np.arange(2 * blk)[None, :]
    delta = blk + i - c
    band = (delta >= 0) & (delta <= blk)
    caps = np.stack([band, band & (c >= blk)])
    return np.where(caps, np.finfo(np.float32).max, NEG_INF).astype(np.float32)


def _dilated_group(q, kv, bias, gi, batch, seq):
    d = DIL_CONFIGS[gi][1]
    blk = ATT_BLK
    nb = seq // d // blk
    hpb = min(ATT_HEADS, ATT_TILES_PER_STEP // d)
    nhb = ATT_HEADS // hpb
    band = (1, d, blk, hpb * ATT_HEAD_DIM)
    prev = lambda n: jnp.maximum(n - 1, 0)
    return pl.pallas_call(
        functools.partial(_att_kernel, d=d, hpb=hpb),
        out_shape=(jax.ShapeDtypeStruct((ATT_HEADS // 2, batch * seq, ATT_HEAD_DIM), jnp.int32),
                   jax.ShapeDtypeStruct((nhb, batch * seq, LANES), F32)),
        grid=(batch, nb, nhb),
        in_specs=[
            pl.BlockSpec(band, lambda b, n, hb: (b, 0, n, hb)),
            pl.BlockSpec(band, lambda b, n, hb: (b, 0, prev(n), hb)),
            pl.BlockSpec(band, lambda b, n, hb: (b, 0, n, hb)),
            pl.BlockSpec(band, lambda b, n, hb: (b, 0, prev(n), nhb + hb)),
            pl.BlockSpec(band, lambda b, n, hb: (b, 0, n, nhb + hb)),
            pl.BlockSpec((1, hpb, blk, 2 * blk), lambda b, n, hb: (gi, hb, 0, 0)),
            pl.BlockSpec((2, blk, 2 * blk), lambda b, n, hb: (0, 0, 0)),
        ],
        out_specs=(pl.BlockSpec((hpb // 2, d * blk, ATT_HEAD_DIM), lambda b, n, hb: (hb, b * nb + n, 0)),
                   pl.BlockSpec((1, d * blk, LANES), lambda b, n, hb: (hb, b * nb + n, 0))),
        compiler_params=_params(("parallel", "parallel", "parallel")),
        name="dilated_attention_g%d" % gi,
    )(q, kv, kv, kv, kv, bias, jnp.asarray(_band_caps()))


def _merge_out_kernel(o0_ref, o1_ref, o2_ref, l0_ref, l1_ref, l2_ref, x_ref, w_ref, out_ref, ma_ref, mb_ref):
    s = pl.program_id(0)

    def step(src_ref, dst_ref):
        out_ref[...] = x_ref[...] + jnp.dot(src_ref[...], w_ref[...], preferred_element_type=F32)
        l0 = jnp.sum(l0_ref[...], axis=0)
        l1 = jnp.sum(l1_ref[...], axis=0)
        l2 = jnp.sum(l2_ref[...], axis=0)
        mx = jnp.maximum(jnp.maximum(l0, l1), l2)
        e0 = jnp.exp(l0 - mx)
        e1 = jnp.exp(l1 - mx)
        e2 = jnp.exp(l2 - mx)
        den = e0 + e1 + e2
        w0 = e0 / den
        w1 = e1 / den
        w2 = e2 / den
        tm = l0.shape[0]
        for h in range(ATT_HEADS):
            hs = slice(h * ATT_HEAD_DIM, (h + 1) * ATT_HEAD_DIM)
            b0 = jnp.broadcast_to(w0[:, h:h + 1], (tm, ATT_HEAD_DIM))
            b1 = jnp.broadcast_to(w1[:, h:h + 1], (tm, ATT_HEAD_DIM))
            b2 = jnp.broadcast_to(w2[:, h:h + 1], (tm, ATT_HEAD_DIM))
            o0, o1, o2 = (
                pltpu.unpack_elementwise(o_ref[h // 2], index=h % 2, packed_dtype=BF16, unpacked_dtype=F32)
                for o_ref in (o0_ref, o1_ref, o2_ref))
            dst_ref[:, hs] = (b0 * o0 + b1 * o1 + b2 * o2).astype(dst_ref.dtype)

    @pl.when(s == 0)
    def _():
        mb_ref[...] = jnp.zeros_like(mb_ref)

    @pl.when(s % 2 == 0)
    def _():
        step(mb_ref, ma_ref)

    @pl.when(s % 2 == 1)
    def _():
        step(ma_ref, mb_ref)


def _merge_out(outs, lses, x, w, layer, *, tm):
    t, d = x.shape
    nt = t // tm
    cur = lambda s: jnp.minimum(s, nt - 1)
    prv = lambda s: jnp.maximum(s - 1, 0)
    heads = pl.BlockSpec((ATT_HEADS // 2, tm, ATT_HEAD_DIM), lambda s: (0, cur(s), 0))
    narrow = [pl.BlockSpec((l.shape[0], tm, LANES), lambda s: (0, cur(s), 0)) for l in lses]
    return pl.pallas_call(
        _merge_out_kernel,
        out_shape=jax.ShapeDtypeStruct((t, d), F32),
        grid=(nt + 1,),
        in_specs=[heads, heads, heads, *narrow,
                  pl.BlockSpec((tm, d), lambda s: (prv(s), 0)),
                  pl.BlockSpec((None, ATT_WIDTH, d), lambda s: (layer, 0, 0))],
        out_specs=pl.BlockSpec((tm, d), lambda s: (prv(s), 0)),
        scratch_shapes=[pltpu.VMEM((tm, ATT_WIDTH), BF16), pltpu.VMEM((tm, ATT_WIDTH), BF16)],
        compiler_params=_params(("arbitrary",)),
        name="merge_out",
    )(*outs, *lses, x, w)


def _as_residue_major(a, batch, seq):
    return a.reshape(batch, 1, seq, a.shape[1]) if a.ndim == 2 else a


def kernel(x, g_mix, g_ffn, w_ret_in, w_ret_out, g_kv, w_kv, w_att_q, w_att_out, rel_bias, w_ffn_in, w_ffn_out, g_final):
    batch, seq, d = x.shape
    t = batch * seq
    x = x.reshape(t, d)
    tn = 1024
    ret_dims = dict(batch=batch, seq=seq, tm=1024, tn=tn)
    kv_dims = dict(batch=batch, seq=seq, tm=1024, tn=tn, single_buffer_x=True)
    q_dims = dict(batch=batch, seq=seq, tm=512, tn=tn)

    inv = (1.0 / ROPE_BASE ** np.linspace(0.0, 1.0, ROPE_HALF)).astype(np.float32)
    ang = jnp.arange(seq, dtype=F32)[:, None] * jnp.asarray(inv)[None, :]
    rope = (jnp.cos(ang), jnp.sin(ang))

    qt, vt, wt = RET_NQ // tn, RET_NV // tn, ATT_WIDTH // tn
    ret_outs = (
        _Out(2 * RET_NQ, 1, "rope", (_Seg(0, qt, 0, 1.0), _Seg(qt, qt, qt, RET_QK_DIM ** -0.5))),
        _Out(RET_NV, 1, "cast", (_Seg(2 * qt, vt, 0, 1.0),)),
        _Out(RET_NV, 1, "silu", (_Seg(2 * qt + vt, vt, 0, 1.0),)),
    )
    kv_outs = tuple(
        _Out(2 * ATT_WIDTH, dd, "cast", (_Seg(gi * wt, wt, 0, 1.0), _Seg((N_GROUPS + gi) * wt, wt, wt, 1.0)))
        for gi, (_, dd) in enumerate(DIL_CONFIGS))
    q_outs = tuple(
        _Out(ATT_WIDTH, dd, "cast", (_Seg(gi * wt, wt, 0, 1.0),)) for gi, (_, dd) in enumerate(DIL_CONFIGS))

    def whole(w):
        return w.reshape((1, -1) + w.shape[-1:])

    ret_in = [w_ret_in[0].astype(BF16)[None], None]
    ret_out = [w_ret_out[0].astype(BF16)[None], None]
    ffn_in = [None, None, None, w_ffn_in[3].astype(BF16)[None]]
    ffn_out = [None, None, None, w_ffn_out[3].astype(BF16)[None]]
    w_kv_b = att_q = att_out = None

    kvs = bias = None
    for l in range(DEPTH):
        if l == 0:
            (qk, v, gate), (ret_in[1], ret_out[1], ffn_in[0], ffn_out[0]) = _norm_matmul(
                x, g_mix[l], ret_in[l], 0, ret_outs, rope=rope, **ret_dims,
                casts=[(w_ret_in, 1), (w_ret_out, 1), (w_ffn_in, 0), (w_ffn_out, 0)])
            y = _retention(qk, v, gate, batch, seq, hps=4)
            x, (ffn_in[2],) = _matmul_residual(y, ret_out[l], 0, x, tm=512, tn=1024, casts=[(w_ffn_in, 2)])
        elif l == 1:
            (qk, v, gate), (ffn_in[1], ffn_out[1], w_kv_b, att_q) = _norm_matmul(
                x, g_mix[l], ret_in[l], 0, ret_outs, rope=rope, **ret_dims,
                casts=[(w_ffn_in, 1), (w_ffn_out, 1), (whole(w_kv), 0), (whole(w_att_q), 0)])
            att_q = att_q.reshape(w_att_q.shape)
            y = _retention(qk, v, gate, batch, seq, hps=4)
            x, (ffn_out[2], att_out) = _matmul_residual(
                y, ret_out[l], 0, x, tm=512, tn=1024, casts=[(w_ffn_out, 2), (whole(w_att_out), 0)])
            att_out = att_out.reshape(w_att_out.shape)
        else:
            j = l - N_RET_LAYERS
            if j == 0:
                kvs, _ = _norm_matmul(x, g_kv, w_kv_b, 0, kv_outs, **kv_dims)
                bias = _rel_bias_tiles(rel_bias)
            qs, _ = _norm_matmul(x, g_mix[l], att_q, j, q_outs, **q_dims)
            outs, lses = [], []
            for gi in range(N_GROUPS):
                o, lse = _dilated_group(_as_residue_major(qs[gi], batch, seq),
                                        _as_residue_major(kvs[gi], batch, seq), bias, gi, batch, seq)
                outs.append(o)
                lses.append(lse)
            x = _merge_out(outs, lses, x, att_out, j, tm=256)
        x = _ffn(x, g_ffn[l], ffn_in[l], ffn_out[l], 0, tm=1024, tf=512,
                 final_gain=g_final if l == DEPTH - 1 else None)
    return x.reshape(batch, seq, d)
```

```python
import functools
from typing import NamedTuple

import numpy as np
import jax
import jax.numpy as jnp
from jax import lax
from jax.experimental import pallas as pl
from jax.experimental.pallas import tpu as pltpu

F32 = jnp.float32
BF16 = jnp.bfloat16

D_MODEL = 2048
DEPTH = 4
N_RET_LAYERS = DEPTH // 2

RET_HEADS = 8
RET_QK_DIM = D_MODEL // RET_HEADS
RET_V_DIM = 2 * RET_QK_DIM
RET_NQ = RET_HEADS * RET_QK_DIM
RET_NV = RET_HEADS * RET_V_DIM
ROPE_BASE = 10000.0
ROPE_HALF = RET_QK_DIM // 2

DIL_CONFIGS = ((128, 1), (512, 4), (2048, 16))
N_GROUPS = len(DIL_CONFIGS)
ATT_HEAD_DIM = 128
ATT_HEADS = D_MODEL // ATT_HEAD_DIM
ATT_WIDTH = ATT_HEADS * ATT_HEAD_DIM
ATT_BLK = 128
ATT_TILES_PER_STEP = 64
REL_BUCKETS = 32
REL_MAX_DIST = 2048
FFN_DIM = -(-8 * D_MODEL // (3 * 256)) * 256
NORM_EPS = 1e-6
NEG_INF = -1e30

LANES = 128
RET_CHUNK = 256
RMS_ROWS = 128
X_PARTS = 4
VMEM_PHYSICAL = 64 * 1024 * 1024
VMEM_LIMIT = VMEM_PHYSICAL * 7 // 8


def _params(semantics):
    return pltpu.CompilerParams(dimension_semantics=semantics, vmem_limit_bytes=VMEM_LIMIT)


def _rms_rows(x_refs, g_ref, dst_ref, copy_ref=None, stage_ref=None):
    g = g_ref[...]
    part = x_refs[0].shape[0]

    for q, x_ref in enumerate(x_refs):
        def body(c, carry, q=q, x_ref=x_ref):
            r0 = pl.multiple_of(c * RMS_ROWS, RMS_ROWS)
            x = x_ref[pl.ds(r0, RMS_ROWS), :]
            ms = jnp.mean(x * x, axis=-1, keepdims=True)
            y = x * lax.rsqrt(ms + NORM_EPS) * g
            rows = pl.ds(q * part + r0, RMS_ROWS)
            dst_ref[rows, :] = y.astype(dst_ref.dtype)
            if copy_ref is not None:
                copy_ref[rows, :] = x
            if stage_ref is not None:
                for k in range(stage_ref.shape[0]):
                    stage_ref[k, rows, :] = y[:, k * LANES:(k + 1) * LANES]
            return carry

        lax.fori_loop(0, part // RMS_ROWS, body, 0)


def _cast_specs(casts, n_outer, n_inner, outer, inner):
    in_specs, out_specs, out_shapes = [], [], []
    for w, wl in casts:
        rows, cols = w.shape[1:]
        m = max(k for k in range(1, n_inner + 1) if cols % (k * LANES) == 0)
        blk = (None, rows // n_outer, cols // m)
        in_specs.append(pl.BlockSpec(
            blk, lambda *g, wl=wl, m=m: (wl, outer(*g), jnp.minimum(inner(*g), m - 1))))
        out_specs.append(pl.BlockSpec(
            blk, lambda *g, m=m: (0, outer(*g), jnp.minimum(inner(*g), m - 1))))
        out_shapes.append(jax.ShapeDtypeStruct((1, rows, cols), BF16))
    return in_specs, out_specs, out_shapes


def _run_casts(src_refs, dst_refs):
    for src_ref, dst_ref in zip(src_refs, dst_refs):
        dst_ref[...] = src_ref[...].astype(dst_ref.dtype)


class _Seg(NamedTuple):
    j0: int
    nt: int
    cb0: int
    scale: float


class _Out(NamedTuple):
    ncols: int
    dilation: int
    epilogue: str
    segs: tuple


def _col_block(out, j):
    cb = out.segs[0].cb0
    for seg in out.segs:
        cb = jnp.where(j >= seg.j0, seg.cb0 + jnp.minimum(j - seg.j0, seg.nt - 1), cb)
    return cb


def _norm_matmul_kernel(*refs, outs, has_rope, dilations, n_cast, x_parts):
    x_refs = refs[:x_parts]
    g_ref, w_ref = refs[x_parts:x_parts + 2]
    pos = x_parts + 2
    if has_rope:
        cos_ref, sin_ref = refs[pos:pos + 2]
        pos += 2
    cast_src = refs[pos:pos + n_cast]
    pos += n_cast
    out_refs = refs[pos:pos + len(outs)]
    pos += len(outs)
    cast_dst = refs[pos:pos + n_cast]
    pos += n_cast
    h_refs = {1: refs[pos]}
    for k, d in enumerate(dilations):
        h_refs[d] = refs[pos + 1 + k]
    stage_ref = refs[pos + 1 + len(dilations)] if dilations else None
    j = pl.program_id(1)
    tm = x_parts * x_refs[0].shape[0]

    @pl.when(j == 0)
    def _():
        _rms_rows(x_refs, g_ref, h_refs[1], stage_ref=stage_ref)
        for d in dilations:
            n_r = tm // d
            for r in range(d):
                for k in range(stage_ref.shape[0]):
                    h_refs[d][r * n_r:(r + 1) * n_r, k * LANES:(k + 1) * LANES] = (
                        stage_ref[k, pl.ds(r, n_r, stride=d), :].astype(BF16))

    def emit(out, o_ref):
        _run_casts(cast_src, cast_dst)
        acc = jnp.dot(h_refs[out.dilation][...], w_ref[...], preferred_element_type=F32)
        tn = acc.shape[1]
        if out.dilation > 1:
            n_r = tm // out.dilation
            for r in range(out.dilation):
                o_ref[0, r] = acc[r * n_r:(r + 1) * n_r, :].astype(o_ref.dtype)
        elif out.epilogue == "rope":
            scale = F32(out.segs[0].scale)
            for seg in out.segs[1:]:
                scale = jnp.where(j >= seg.j0, F32(seg.scale), scale)
            cos = cos_ref[...]
            sin = sin_ref[...]
            for c in range(0, tn, 2 * ROPE_HALF):
                t1 = acc[:, c:c + ROPE_HALF] * scale
                t2 = acc[:, c + ROPE_HALF:c + 2 * ROPE_HALF] * scale
                o_ref[:, c:c + ROPE_HALF] = (t1 * cos - t2 * sin).astype(o_ref.dtype)
                o_ref[:, c + ROPE_HALF:c + 2 * ROPE_HALF] = (t2 * cos + t1 * sin).astype(o_ref.dtype)
        elif out.epilogue == "silu":
            o_ref[...] = (acc * (1.0 / (1.0 + jnp.exp(-acc)))).astype(o_ref.dtype)
        else:
            o_ref[...] = acc.astype(o_ref.dtype)

    for out, o_ref in zip(outs, out_refs):
        hit = None
        for seg in out.segs:
            inside = (j >= seg.j0) & (j < seg.j0 + seg.nt)
            hit = inside if hit is None else hit | inside
        pl.when(hit)(functools.partial(emit, out, o_ref))


def _norm_matmul(x, gain, w, layer, outs, *, batch, seq, tm, tn, rope=None, single_buffer_x=False, casts=()):
    t, d = x.shape
    n = w.shape[2]
    tiles_per_batch = seq // tm
    dilations = tuple(sorted({o.dilation for o in outs} - {1}))
    ni, nj = t // tm, n // tn
    if single_buffer_x:
        x_parts = 1
        x_specs = [pl.BlockSpec((tm, d), lambda i, j: (i, 0), pipeline_mode=pl.Buffered(1))]
    else:
        x_parts = min(X_PARTS, nj - 1)
        x_specs = [
            pl.BlockSpec((tm // x_parts, d),
                         lambda i, j, q=q: (x_parts * jnp.minimum(i + (j >= nj - x_parts + q), ni - 1) + q, 0))
            for q in range(x_parts)]
    in_specs = x_specs + [
        pl.BlockSpec((1, d), lambda i, j: (0, 0)),
        pl.BlockSpec((None, d, tn), lambda i, j: (layer, 0, j)),
    ]
    args = [x] * x_parts + [gain.reshape(1, d), w]
    if rope is not None:
        in_specs += [pl.BlockSpec((tm, ROPE_HALF), lambda i, j: (i % tiles_per_batch, 0))] * 2
        args += list(rope)
    cast_in, cast_out, cast_shapes = _cast_specs(casts, t // tm, n // tn, lambda i, j: i, lambda i, j: j)
    in_specs += cast_in
    args += [w_ for w_, _ in casts]
    out_shapes, out_specs = [], []
    for out in outs:
        dd = out.dilation
        if dd == 1:
            out_shapes.append(jax.ShapeDtypeStruct((t, out.ncols), BF16))
            out_specs.append(pl.BlockSpec((tm, tn), lambda i, j, out=out: (i, _col_block(out, j))))
        else:
            out_shapes.append(jax.ShapeDtypeStruct((batch, dd, seq // dd, out.ncols), BF16))
            out_specs.append(pl.BlockSpec(
                (1, dd, tm // dd, tn),
                lambda i, j, out=out: (i // tiles_per_batch, 0, i % tiles_per_batch, _col_block(out, j))))
    scratch = [pltpu.VMEM((tm, d), BF16) for _ in range(1 + len(dilations))]
    if dilations:
        scratch.append(pltpu.VMEM((d // LANES, tm, LANES), F32))
    res = pl.pallas_call(
        functools.partial(_norm_matmul_kernel, outs=tuple(outs), has_rope=rope is not None,
                          dilations=dilations, n_cast=len(casts), x_parts=x_parts),
        out_shape=tuple(out_shapes + cast_shapes),
        grid=(t // tm, n // tn),
        in_specs=in_specs,
        out_specs=tuple(out_specs + cast_out),
        scratch_shapes=scratch,
        compiler_params=_params(("parallel", "arbitrary")),
        name="norm_matmul",
    )(*args)
    return res[:len(outs)], res[len(outs):]


def _retention_kernel(lg_ref, cd_ref, q_ref, k_ref, v_ref, g_ref, y_ref, r_ref, dm_ref, qd_ref, kd_ref, *, hps):
    hp = pl.program_id(1)
    n = pl.program_id(2)
    c = q_ref.shape[0]
    dk, dv = RET_QK_DIM, RET_V_DIM

    @pl.when(n == 0)
    def _():
        r_ref[...] = jnp.zeros_like(r_ref)
        i = lax.broadcasted_iota(jnp.int32, (c, c), 0)
        jj = lax.broadcasted_iota(jnp.int32, (c, c), 1)
        diff = i - jj
        row = lax.broadcasted_iota(jnp.int32, (c, LANES), 0).astype(F32)
        for hh in range(hps):
            lg = lg_ref[hp * hps + hh]
            dm_ref[hh] = jnp.where(diff >= 0, jnp.exp(lg * jnp.maximum(diff, 0).astype(F32)), 0.0)
            qd_ref[hh] = jnp.exp(lg * (row + 1.0))
            kd_ref[hh] = jnp.exp(lg * (F32(c - 1) - row))

    for hh in range(hps):
        q = q_ref[:, hh * dk:(hh + 1) * dk]
        k = k_ref[:, hh * dk:(hh + 1) * dk]
        v = v_ref[:, hh * dv:(hh + 1) * dv]
        s = lax.dot_general(q, k, (((1,), (1,)), ((), ())), preferred_element_type=F32) * dm_ref[hh]
        inner = jnp.dot(s.astype(BF16), v, preferred_element_type=F32)
        r_old = r_ref[hh]
        cross = jnp.dot(q, r_old.astype(BF16), preferred_element_type=F32)
        qd = qd_ref[hh]
        kd = kd_ref[hh]
        k_dec = jnp.concatenate(
            [(k[:, o:o + LANES].astype(F32) * kd).astype(BF16) for o in range(0, dk, LANES)], axis=1)
        r_ref[hh] = r_old * cd_ref[hp * hps + hh] + lax.dot_general(
            k_dec, v, (((0,), (0,)), ((), ())), preferred_element_type=F32)

        y = jnp.concatenate(
            [inner[:, o:o + LANES] + cross[:, o:o + LANES] * qd for o in range(0, dv, LANES)], axis=1)
        ms = jnp.mean(y * y, axis=-1, keepdims=True)
        gate = g_ref[:, hh * dv:(hh + 1) * dv].astype(F32)
        y_ref[:, hh * dv:(hh + 1) * dv] = ((y * lax.rsqrt(ms + NORM_EPS)) * gate).astype(y_ref.dtype)


def _retention(qk, v, gate, batch, seq, *, hps):
    t = qk.shape[0]
    c = RET_CHUNK
    nc = seq // c
    nhp = RET_HEADS // hps
    log_g = np.log(1.0 - 2.0 ** (-5.0 - np.arange(RET_HEADS))).astype(np.float32)
    c_dec = np.exp(log_g * c).astype(np.float32)
    smem = pl.BlockSpec(memory_space=pltpu.SMEM)
    qk_shape = (c, hps * RET_QK_DIM)
    v_shape = (c, hps * RET_V_DIM)
    return pl.pallas_call(
        functools.partial(_retention_kernel, hps=hps),
        out_shape=jax.ShapeDtypeStruct((t, RET_NV), BF16),
        grid=(batch, nhp, nc),
        in_specs=[
            smem, smem,
            pl.BlockSpec(qk_shape, lambda b, h, n: (b * nc + n, h)),
            pl.BlockSpec(qk_shape, lambda b, h, n: (b * nc + n, nhp + h)),
            pl.BlockSpec(v_shape, lambda b, h, n: (b * nc + n, h)),
            pl.BlockSpec(v_shape, lambda b, h, n: (b * nc + n, h)),
        ],
        out_specs=pl.BlockSpec(v_shape, lambda b, h, n: (b * nc + n, h)),
        scratch_shapes=[
            pltpu.VMEM((hps, RET_QK_DIM, RET_V_DIM), F32),
            pltpu.VMEM((hps, c, c), F32),
            pltpu.VMEM((hps, c, LANES), F32),
            pltpu.VMEM((hps, c, LANES), F32),
        ],
        compiler_params=_params(("parallel", "parallel", "arbitrary")),
        name="retention",
    )(jnp.asarray(log_g), jnp.asarray(c_dec), qk, qk, v, gate)


def _matmul_residual_kernel(*refs, n_cast):
    y_ref, w_ref, x_ref = refs[:3]
    cast_src = refs[3:3 + n_cast]
    o_ref = refs[3 + n_cast]
    cast_dst = refs[4 + n_cast:]
    o_ref[...] = x_ref[...] + jnp.dot(y_ref[...], w_ref[...], preferred_element_type=F32)
    _run_casts(cast_src, cast_dst)


def _matmul_residual(y, w, layer, x, *, tm, tn, casts=()):
    t, k = y.shape
    n = w.shape[2]
    cast_in, cast_out, cast_shapes = _cast_specs(casts, t // tm, n // tn, lambda j, i: i, lambda j, i: j)
    res = pl.pallas_call(
        functools.partial(_matmul_residual_kernel, n_cast=len(casts)),
        out_shape=(jax.ShapeDtypeStruct((t, n), F32), *cast_shapes),
        grid=(n // tn, t // tm),
        in_specs=[
            pl.BlockSpec((tm, k), lambda j, i: (i, 0)),
            pl.BlockSpec((None, k, tn), lambda j, i: (layer, 0, j)),
            pl.BlockSpec((tm, tn), lambda j, i: (i, j)),
            *cast_in,
        ],
        out_specs=(pl.BlockSpec((tm, tn), lambda j, i: (i, j)), *cast_out),
        compiler_params=_params(("parallel", "parallel")),
        name="matmul_residual",
    )(y, w, x, *[w_ for w_, _ in casts])
    return res[0], res[1:]


def _ffn_kernel(*refs, final_norm):
    if final_norm:
        x_ref, g_ref, w1_ref, w2_ref, wo_ref, gf_ref, o_ref, h_ref = refs
    else:
        x_ref, g_ref, w1_ref, w2_ref, wo_ref, o_ref, h_ref = refs
    f = pl.program_id(1)

    @pl.when(f == 0)
    def _():
        _rms_rows([x_ref], g_ref, h_ref, copy_ref=o_ref)

    h = h_ref[...]
    z1 = jnp.dot(h, w1_ref[...], preferred_element_type=F32)
    z2 = jnp.dot(h, w2_ref[...], preferred_element_type=F32)
    a = (z1 * (1.0 / (1.0 + jnp.exp(-z1)))) * z2
    o_ref[...] += jnp.dot(a.astype(BF16), wo_ref[...], preferred_element_type=F32)

    if final_norm:
        @pl.when(f == pl.num_programs(1) - 1)
        def _():
            gf = gf_ref[...]

            def body(c, carry):
                r0 = pl.multiple_of(c * RMS_ROWS, RMS_ROWS)
                x = o_ref[pl.ds(r0, RMS_ROWS), :]
                ms = jnp.mean(x * x, axis=-1, keepdims=True)
                o_ref[pl.ds(r0, RMS_ROWS), :] = x * lax.rsqrt(ms + NORM_EPS) * gf
                return carry

            lax.fori_loop(0, o_ref.shape[0] // RMS_ROWS, body, 0)


def _ffn(x, gain, w_in, w_out, layer, *, tm, tf, final_gain=None):
    t, d = x.shape
    nf = FFN_DIM // tf
    in_specs = [
        pl.BlockSpec((tm, d), lambda i, f: (i, 0)),
        pl.BlockSpec((1, d), lambda i, f: (0, 0)),
        pl.BlockSpec((None, d, tf), lambda i, f: (layer, 0, f)),
        pl.BlockSpec((None, d, tf), lambda i, f: (layer, 0, nf + f)),
        pl.BlockSpec((None, tf, d), lambda i, f: (layer, f, 0)),
    ]
    args = [x, gain.reshape(1, d), w_in, w_in, w_out]
    if final_gain is not None:
        in_specs.append(pl.BlockSpec((1, d), lambda i, f: (0, 0)))
        args.append(final_gain.reshape(1, d))
    return pl.pallas_call(
        functools.partial(_ffn_kernel, final_norm=final_gain is not None),
        out_shape=jax.ShapeDtypeStruct((t, d), F32),
        grid=(t // tm, nf),
        in_specs=in_specs,
        out_specs=pl.BlockSpec((tm, d), lambda i, f: (i, 0)),
        scratch_shapes=[pltpu.VMEM((tm, d), BF16)],
        compiler_params=_params(("parallel", "arbitrary")),
        name="ffn",
    )(*args)


def _t5_bucket_table(dist):
    n = np.maximum(dist, 0)
    max_exact = REL_BUCKETS // 2
    large = max_exact + (np.log(np.maximum(n, 1) / max_exact) / np.log(REL_MAX_DIST / max_exact)
                         * (REL_BUCKETS - max_exact)).astype(np.int32)
    large = np.minimum(large, REL_BUCKETS - 1)
    return np.where(n < max_exact, n, large).astype(np.int32)


def _bias_kernel(tab_ref, bucket_ref, o_ref):
    col = pl.program_id(0) * ATT_HEADS + pl.program_id(1)
    bucket = bucket_ref[0]
    acc = jnp.zeros(bucket.shape, F32)
    for b in range(REL_BUCKETS):
        acc = jnp.where(bucket == b, tab_ref[b, col], acc)
    o_ref[0, 0] = acc * float(np.log2(np.e))


def _rel_bias_tiles(rel_bias):
    blk = ATT_BLK
    i = np.arange(blk)[:, None]
    c = np.arange(2 * blk)[None, :]
    delta = blk + i - c
    buckets = np.stack([_t5_bucket_table(np.maximum(delta, 0) * d) for _, d in DIL_CONFIGS])
    return pl.pallas_call(
        _bias_kernel,
        out_shape=jax.ShapeDtypeStruct((N_GROUPS, ATT_HEADS, blk, 2 * blk), F32),
        grid=(N_GROUPS, ATT_HEADS),
        in_specs=[
            pl.BlockSpec(memory_space=pltpu.SMEM),
            pl.BlockSpec((1, blk, 2 * blk), lambda g, h: (g, 0, 0)),
        ],
        out_specs=pl.BlockSpec((1, 1, blk, 2 * blk), lambda g, h: (g, h, 0, 0)),
        compiler_params=_params(("parallel", "parallel")),
        name="rel_bias",
    )(rel_bias, jnp.asarray(buckets))


def _att_kernel(q_ref, kp_ref, kc_ref, vp_ref, vc_ref, bias_ref, cap_ref, o_ref, lse_ref, *, d, hpb):
    n = pl.program_id(1)
    hb = pl.program_id(2)
    blk = ATT_BLK
    lane = lax.broadcasted_iota(jnp.int32, (blk, LANES), 1)
    log2e = float(np.log2(np.e))
    scale = ATT_HEAD_DIM ** -0.5 * log2e
    first = jnp.where(n > 0, 0, 1)

    def residue(r):
        rows = pl.ds(r, blk, stride=d) if d > 1 else slice(None)
        lse_tile = jnp.zeros((blk, LANES), F32)
        pair = []
        for hl in range(hpb):
            hs = slice(hl * ATT_HEAD_DIM, (hl + 1) * ATT_HEAD_DIM)
            q = q_ref[0, r, :, hs]
            kb = jnp.concatenate([kp_ref[0, r, :, hs], kc_ref[0, r, :, hs]], axis=0)
            vb = jnp.concatenate([vp_ref[0, r, :, hs], vc_ref[0, r, :, hs]], axis=0)
            s = lax.dot_general(q, kb, (((1,), (1,)), ((), ())), preferred_element_type=F32)
            s = jnp.minimum(s * scale + bias_ref[0, hl], cap_ref[first])
            m = jnp.max(s, axis=-1, keepdims=True)
            p = jnp.exp2(s - m)
            den = jnp.sum(p, axis=-1, keepdims=True)
            pair.append(jnp.dot(p.astype(BF16), vb, preferred_element_type=F32) / den)
            if len(pair) == 2:
                o_ref[hl // 2, rows, :] = pltpu.pack_elementwise(pair, packed_dtype=BF16)
                pair = []
            lse_tile = jnp.where(lane == hb * hpb + hl, (m + jnp.log2(den)) * (1.0 / log2e), lse_tile)
        lse_ref[0, rows, :] = lse_tile

    rpi = min(d, max(1, 2 * ATT_HEADS // hpb))

    def body(it, carry):
        for rr in range(rpi):
            residue(it * rpi + rr)
        return carry

    if d == rpi:
        body(0, 0)
    else:
        lax.fori_loop(0, d // rpi, body, 0)


def _band_caps():
    blk = ATT_BLK
    i = np.arange(blk)[:, None]
    c = np.arange(2 * blk)[None, :]
    delta = blk + i - c
    band = (delta >= 0) & (delta <= blk)
    caps = np.stack([band, band & (c >= blk)])
    return np.where(caps, np.finfo(np.float32).max, NEG_INF).astype(np.float32)


def _dilated_group(q, kv, bias, gi, batch, seq):
    d = DIL_CONFIGS[gi][1]
    blk = ATT_BLK
    nb = seq // d // blk
    hpb = min(ATT_HEADS, ATT_TILES_PER_STEP // d)
    nhb = ATT_HEADS // hpb
    band = (1, d, blk, hpb * ATT_HEAD_DIM)
    prev = lambda n: jnp.maximum(n - 1, 0)
    return pl.pallas_call(
        functools.partial(_att_kernel, d=d, hpb=hpb),
        out_shape=(jax.ShapeDtypeStruct((ATT_HEADS // 2, batch * seq, ATT_HEAD_DIM), jnp.int32),
                   jax.ShapeDtypeStruct((nhb, batch * seq, LANES), F32)),
        grid=(batch, nb, nhb),
        in_specs=[
            pl.BlockSpec(band, lambda b, n, hb: (b, 0, n, hb)),
            pl.BlockSpec(band, lambda b, n, hb: (b, 0, prev(n), hb)),
            pl.BlockSpec(band, lambda b, n, hb: (b, 0, n, hb)),
            pl.BlockSpec(band, lambda b, n, hb: (b, 0, prev(n), nhb + hb)),
            pl.BlockSpec(band, lambda b, n, hb: (b, 0, n, nhb + hb)),
            pl.BlockSpec((1, hpb, blk, 2 * blk), lambda b, n, hb: (gi, hb, 0, 0)),
            pl.BlockSpec((2, blk, 2 * blk), lambda b, n, hb: (0, 0, 0)),
        ],
        out_specs=(pl.BlockSpec((hpb // 2, d * blk, ATT_HEAD_DIM), lambda b, n, hb: (hb, b * nb + n, 0)),
                   pl.BlockSpec((1, d * blk, LANES), lambda b, n, hb: (hb, b * nb + n, 0))),
        compiler_params=_params(("parallel", "parallel", "parallel")),
        name="dilated_attention_g%d" % gi,
    )(q, kv, kv, kv, kv, bias, jnp.asarray(_band_caps()))


def _merge_out_kernel(o0_ref, o1_ref, o2_ref, l0_ref, l1_ref, l2_ref, x_ref, w_ref, out_ref, ma_ref, mb_ref):
    s = pl.program_id(0)

    def step(src_ref, dst_ref):
        out_ref[...] = x_ref[...] + jnp.dot(src_ref[...], w_ref[...], preferred_element_type=F32)
        l0 = jnp.sum(l0_ref[...], axis=0)
        l1 = jnp.sum(l1_ref[...], axis=0)
        l2 = jnp.sum(l2_ref[...], axis=0)
        mx = jnp.maximum(jnp.maximum(l0, l1), l2)
        e0 = jnp.exp(l0 - mx)
        e1 = jnp.exp(l1 - mx)
        e2 = jnp.exp(l2 - mx)
        den = e0 + e1 + e2
        w0 = e0 / den
        w1 = e1 / den
        w2 = e2 / den
        tm = l0.shape[0]
        for h in range(ATT_HEADS):
            hs = slice(h * ATT_HEAD_DIM, (h + 1) * ATT_HEAD_DIM)
            b0 = jnp.broadcast_to(w0[:, h:h + 1], (tm, ATT_HEAD_DIM))
            b1 = jnp.broadcast_to(w1[:, h:h + 1], (tm, ATT_HEAD_DIM))
            b2 = jnp.broadcast_to(w2[:, h:h + 1], (tm, ATT_HEAD_DIM))
            o0, o1, o2 = (
                pltpu.unpack_elementwise(o_ref[h // 2], index=h % 2, packed_dtype=BF16, unpacked_dtype=F32)
                for o_ref in (o0_ref, o1_ref, o2_ref))
            dst_ref[:, hs] = (b0 * o0 + b1 * o1 + b2 * o2).astype(dst_ref.dtype)

    @pl.when(s == 0)
    def _():
        mb_ref[...] = jnp.zeros_like(mb_ref)

    @pl.when(s % 2 == 0)
    def _():
        step(mb_ref, ma_ref)

    @pl.when(s % 2 == 1)
    def _():
        step(ma_ref, mb_ref)


def _merge_out(outs, lses, x, w, layer, *, tm):
    t, d = x.shape
    nt = t // tm
    cur = lambda s: jnp.minimum(s, nt - 1)
    prv = lambda s: jnp.maximum(s - 1, 0)
    heads = pl.BlockSpec((ATT_HEADS // 2, tm, ATT_HEAD_DIM), lambda s: (0, cur(s), 0))
    narrow = [pl.BlockSpec((l.shape[0], tm, LANES), lambda s: (0, cur(s), 0)) for l in lses]
    return pl.pallas_call(
        _merge_out_kernel,
        out_shape=jax.ShapeDtypeStruct((t, d), F32),
        grid=(nt + 1,),
        in_specs=[heads, heads, heads, *narrow,
                  pl.BlockSpec((tm, d), lambda s: (prv(s), 0)),
                  pl.BlockSpec((None, ATT_WIDTH, d), lambda s: (layer, 0, 0))],
        out_specs=pl.BlockSpec((tm, d), lambda s: (prv(s), 0)),
        scratch_shapes=[pltpu.VMEM((tm, ATT_WIDTH), BF16), pltpu.VMEM((tm, ATT_WIDTH), BF16)],
        compiler_params=_params(("arbitrary",)),
        name="merge_out",
    )(*outs, *lses, x, w)


def _as_residue_major(a, batch, seq):
    return a.reshape(batch, 1, seq, a.shape[1]) if a.ndim == 2 else a


def kernel(x, g_mix, g_ffn, w_ret_in, w_ret_out, g_kv, w_kv, w_att_q, w_att_out, rel_bias, w_ffn_in, w_ffn_out, g_final):
    batch, seq, d = x.shape
    t = batch * seq
    x = x.reshape(t, d)
    tn = 1024
    ret_dims = dict(batch=batch, seq=seq, tm=1024, tn=tn)
    kv_dims = dict(batch=batch, seq=seq, tm=1024, tn=tn, single_buffer_x=True)
    q_dims = dict(batch=batch, seq=seq, tm=512, tn=tn)

    inv = (1.0 / ROPE_BASE ** np.linspace(0.0, 1.0, ROPE_HALF)).astype(np.float32)
    ang = jnp.arange(seq, dtype=F32)[:, None] * jnp.asarray(inv)[None, :]
    rope = (jnp.cos(ang), jnp.sin(ang))

    qt, vt, wt = RET_NQ // tn, RET_NV // tn, ATT_WIDTH // tn
    ret_outs = (
        _Out(2 * RET_NQ, 1, "rope", (_Seg(0, qt, 0, 1.0), _Seg(qt, qt, qt, RET_QK_DIM ** -0.5))),
        _Out(RET_NV, 1, "cast", (_Seg(2 * qt, vt, 0, 1.0),)),
        _Out(RET_NV, 1, "silu", (_Seg(2 * qt + vt, vt, 0, 1.0),)),
    )
    kv_outs = tuple(
        _Out(2 * ATT_WIDTH, dd, "cast", (_Seg(gi * wt, wt, 0, 1.0), _Seg((N_GROUPS + gi) * wt, wt, wt, 1.0)))
        for gi, (_, dd) in enumerate(DIL_CONFIGS))
    q_outs = tuple(
        _Out(ATT_WIDTH, dd, "cast", (_Seg(gi * wt, wt, 0, 1.0),)) for gi, (_, dd) in enumerate(DIL_CONFIGS))

    def whole(w):
        return w.reshape((1, -1) + w.shape[-1:])

    ret_in = [w_ret_in[0].astype(BF16)[None], None]
    ret_out = [w_ret_out[0].astype(BF16)[None], None]
    ffn_in = [None] * DEPTH
    ffn_out = [None] * DEPTH
    w_kv_b = att_q = att_out = None

    kvs = bias = None
    for l in range(DEPTH):
        if l == 0:
            (qk, v, gate), (ret_in[1], ret_out[1], ffn_in[0], ffn_out[0]) = _norm_matmul(
                x, g_mix[l], ret_in[l], 0, ret_outs, rope=rope, **ret_dims,
                casts=[(w_ret_in, 1), (w_ret_out, 1), (w_ffn_in, 0), (w_ffn_out, 0)])
            y = _retention(qk, v, gate, batch, seq, hps=4)
            x, (ffn_in[2],) = _matmul_residual(y, ret_out[l], 0, x, tm=512, tn=1024, casts=[(w_ffn_in, 2)])
        elif l == 1:
            (qk, v, gate), (ffn_in[1], ffn_out[1], w_kv_b, att_q) = _norm_matmul(
                x, g_mix[l], ret_in[l], 0, ret_outs, rope=rope, **ret_dims,
                casts=[(w_ffn_in, 1), (w_ffn_out, 1), (whole(w_kv), 0), (whole(w_att_q), 0)])
            att_q = att_q.reshape(w_att_q.shape)
            y = _retention(qk, v, gate, batch, seq, hps=4)
            x, (ffn_out[2], att_out) = _matmul_residual(
                y, ret_out[l], 0, x, tm=512, tn=1024, casts=[(w_ffn_out, 2), (whole(w_att_out), 0)])
            att_out = att_out.reshape(w_att_out.shape)
        else:
            j = l - N_RET_LAYERS
            if j == 0:
                kvs, (ffn_in[3], ffn_out[3]) = _norm_matmul(
                    x, g_kv, w_kv_b, 0, kv_outs, **kv_dims, casts=[(w_ffn_in, 3), (w_ffn_out, 3)])
                bias = _rel_bias_tiles(rel_bias)
            qs, _ = _norm_matmul(x, g_mix[l], att_q, j, q_outs, **q_dims)
            outs, lses = [], []
            for gi in range(N_GROUPS):
                o, lse = _dilated_group(_as_residue_major(qs[gi], batch, seq),
                                        _as_residue_major(kvs[gi], batch, seq), bias, gi, batch, seq)
                outs.append(o)
                lses.append(lse)
            x = _merge_out(outs, lses, x, att_out, j, tm=256)
        x = _ffn(x, g_ffn[l], ffn_in[l], ffn_out[l], 0, tm=1024, tf=512,
                 final_gain=g_final if l == DEPTH - 1 else None)
    return x.reshape(batch, seq, d)
```

```python
import functools
from typing import NamedTuple

import numpy as np
import jax
import jax.numpy as jnp
from jax import lax
from jax.experimental import pallas as pl
from jax.experimental.pallas import tpu as pltpu

F32 = jnp.float32
BF16 = jnp.bfloat16

D_MODEL = 2048
DEPTH = 4
N_RET_LAYERS = DEPTH // 2

RET_HEADS = 8
RET_QK_DIM = D_MODEL // RET_HEADS
RET_V_DIM = 2 * RET_QK_DIM
RET_NQ = RET_HEADS * RET_QK_DIM
RET_NV = RET_HEADS * RET_V_DIM
ROPE_BASE = 10000.0
ROPE_HALF = RET_QK_DIM // 2

DIL_CONFIGS = ((128, 1), (512, 4), (2048, 16))
N_GROUPS = len(DIL_CONFIGS)
ATT_HEAD_DIM = 128
ATT_HEADS = D_MODEL // ATT_HEAD_DIM
ATT_WIDTH = ATT_HEADS * ATT_HEAD_DIM
ATT_BLK = 128
ATT_TILES_PER_STEP = 64
REL_BUCKETS = 32
REL_MAX_DIST = 2048
FFN_DIM = -(-8 * D_MODEL // (3 * 256)) * 256
NORM_EPS = 1e-6
NEG_INF = -1e30

LANES = 128
RET_CHUNK = 256
RMS_ROWS = 128
X_PARTS = 4
VMEM_PHYSICAL = 64 * 1024 * 1024
VMEM_LIMIT = VMEM_PHYSICAL * 7 // 8


def _params(semantics):
    return pltpu.CompilerParams(dimension_semantics=semantics, vmem_limit_bytes=VMEM_LIMIT)


def _rms_rows(x_refs, g_ref, dst_ref, copy_ref=None, stage_ref=None):
    g = g_ref[...]
    part = x_refs[0].shape[0]

    for q, x_ref in enumerate(x_refs):
        def body(c, carry, q=q, x_ref=x_ref):
            r0 = pl.multiple_of(c * RMS_ROWS, RMS_ROWS)
            x = x_ref[pl.ds(r0, RMS_ROWS), :]
            ms = jnp.mean(x * x, axis=-1, keepdims=True)
            y = x * lax.rsqrt(ms + NORM_EPS) * g
            rows = pl.ds(q * part + r0, RMS_ROWS)
            dst_ref[rows, :] = y.astype(dst_ref.dtype)
            if copy_ref is not None:
                copy_ref[rows, :] = x
            if stage_ref is not None:
                for k in range(stage_ref.shape[0]):
                    stage_ref[k, rows, :] = y[:, k * LANES:(k + 1) * LANES]
            return carry

        lax.fori_loop(0, part // RMS_ROWS, body, 0)


def _cast_specs(casts, n_outer, n_inner, outer, inner):
    in_specs, out_specs, out_shapes = [], [], []
    for w, wl, tile in casts:
        rows, cols = w.shape[1:]
        rb = rows // n_outer
        if tile is None:
            m = max(k for k in range(1, n_inner + 1) if cols % (k * LANES) == 0)
            cb = cols // m
            out_specs.append(pl.BlockSpec(
                (None, rb, cb), lambda *g, m=m: (0, outer(*g), jnp.minimum(inner(*g), m - 1))))
            out_shapes.append(jax.ShapeDtypeStruct((1, rows, cols), BF16))
        else:
            m, cb = cols // tile, tile
            assert m <= n_inner
            out_specs.append(pl.BlockSpec(
                (None, None, rb, cb), lambda *g, m=m: (0, jnp.minimum(inner(*g), m - 1), outer(*g), 0)))
            out_shapes.append(jax.ShapeDtypeStruct((1, m, rows, cb), BF16))
        in_specs.append(pl.BlockSpec(
            (None, rb, cb), lambda *g, wl=wl, m=m: (wl, outer(*g), jnp.minimum(inner(*g), m - 1))))
    return in_specs, out_specs, out_shapes


def _col_tiled(w, tn):
    rows, cols = w.shape
    return w.reshape(rows, cols // tn, tn).transpose(1, 0, 2)[None]


def _run_casts(src_refs, dst_refs):
    for src_ref, dst_ref in zip(src_refs, dst_refs):
        dst_ref[...] = src_ref[...].astype(dst_ref.dtype)


class _Seg(NamedTuple):
    j0: int
    nt: int
    cb0: int
    scale: float


class _Out(NamedTuple):
    ncols: int
    dilation: int
    epilogue: str
    segs: tuple


def _col_block(out, j):
    cb = out.segs[0].cb0
    for seg in out.segs:
        cb = jnp.where(j >= seg.j0, seg.cb0 + jnp.minimum(j - seg.j0, seg.nt - 1), cb)
    return cb


def _norm_matmul_kernel(*refs, outs, has_rope, dilations, n_cast, x_parts):
    x_refs = refs[:x_parts]
    g_ref, w_ref = refs[x_parts:x_parts + 2]
    pos = x_parts + 2
    if has_rope:
        cos_ref, sin_ref = refs[pos:pos + 2]
        pos += 2
    cast_src = refs[pos:pos + n_cast]
    pos += n_cast
    out_refs = refs[pos:pos + len(outs)]
    pos += len(outs)
    cast_dst = refs[pos:pos + n_cast]
    pos += n_cast
    h_refs = {1: refs[pos]}
    for k, d in enumerate(dilations):
        h_refs[d] = refs[pos + 1 + k]
    stage_ref = refs[pos + 1 + len(dilations)] if dilations else None
    j = pl.program_id(1)
    tm = x_parts * x_refs[0].shape[0]

    @pl.when(j == 0)
    def _():
        _rms_rows(x_refs, g_ref, h_refs[1], stage_ref=stage_ref)
        for d in dilations:
            n_r = tm // d
            for r in range(d):
                for k in range(stage_ref.shape[0]):
                    h_refs[d][r * n_r:(r + 1) * n_r, k * LANES:(k + 1) * LANES] = (
                        stage_ref[k, pl.ds(r, n_r, stride=d), :].astype(BF16))

    def emit(out, o_ref):
        _run_casts(cast_src, cast_dst)
        acc = jnp.dot(h_refs[out.dilation][...], w_ref[...], preferred_element_type=F32)
        tn = acc.shape[1]
        if out.dilation > 1:
            n_r = tm // out.dilation
            for r in range(out.dilation):
                o_ref[0, r] = acc[r * n_r:(r + 1) * n_r, :].astype(o_ref.dtype)
        elif out.epilogue == "rope":
            scale = F32(out.segs[0].scale)
            for seg in out.segs[1:]:
                scale = jnp.where(j >= seg.j0, F32(seg.scale), scale)
            cos = cos_ref[...]
            sin = sin_ref[...]
            for c in range(0, tn, 2 * ROPE_HALF):
                t1 = acc[:, c:c + ROPE_HALF] * scale
                t2 = acc[:, c + ROPE_HALF:c + 2 * ROPE_HALF] * scale
                o_ref[:, c:c + ROPE_HALF] = (t1 * cos - t2 * sin).astype(o_ref.dtype)
                o_ref[:, c + ROPE_HALF:c + 2 * ROPE_HALF] = (t2 * cos + t1 * sin).astype(o_ref.dtype)
        elif out.epilogue == "silu":
            o_ref[...] = (acc * (1.0 / (1.0 + jnp.exp(-acc)))).astype(o_ref.dtype)
        else:
            o_ref[...] = acc.astype(o_ref.dtype)

    for out, o_ref in zip(outs, out_refs):
        hit = None
        for seg in out.segs:
            inside = (j >= seg.j0) & (j < seg.j0 + seg.nt)
            hit = inside if hit is None else hit | inside
        pl.when(hit)(functools.partial(emit, out, o_ref))


def _norm_matmul(x, gain, w, row_block, outs, *, batch, seq, tm, rope=None, single_buffer_x=False, casts=()):
    t, d = x.shape
    tn = w.shape[3]
    n = w.shape[1] * tn
    tiles_per_batch = seq // tm
    dilations = tuple(sorted({o.dilation for o in outs} - {1}))
    ni, nj = t // tm, n // tn
    if single_buffer_x:
        x_parts = 1
        x_specs = [pl.BlockSpec((tm, d), lambda i, j: (i, 0), pipeline_mode=pl.Buffered(1))]
    else:
        x_parts = min(X_PARTS, nj - 1)
        x_specs = [
            pl.BlockSpec((tm // x_parts, d),
                         lambda i, j, q=q: (x_parts * jnp.minimum(i + (j >= nj - x_parts + q), ni - 1) + q, 0))
            for q in range(x_parts)]
    in_specs = x_specs + [
        pl.BlockSpec((1, d), lambda i, j: (0, 0)),
        pl.BlockSpec((None, None, d, tn), lambda i, j: (0, j, row_block, 0)),
    ]
    args = [x] * x_parts + [gain.reshape(1, d), w]
    if rope is not None:
        in_specs += [pl.BlockSpec((tm, ROPE_HALF), lambda i, j: (i % tiles_per_batch, 0))] * 2
        args += list(rope)
    cast_in, cast_out, cast_shapes = _cast_specs(casts, t // tm, n // tn, lambda i, j: i, lambda i, j: j)
    in_specs += cast_in
    args += [c[0] for c in casts]
    out_shapes, out_specs = [], []
    for out in outs:
        dd = out.dilation
        if dd == 1:
            out_shapes.append(jax.ShapeDtypeStruct((t, out.ncols), BF16))
            out_specs.append(pl.BlockSpec((tm, tn), lambda i, j, out=out: (i, _col_block(out, j))))
        else:
            out_shapes.append(jax.ShapeDtypeStruct((batch, dd, seq // dd, out.ncols), BF16))
            out_specs.append(pl.BlockSpec(
                (1, dd, tm // dd, tn),
                lambda i, j, out=out: (i // tiles_per_batch, 0, i % tiles_per_batch, _col_block(out, j))))
    scratch = [pltpu.VMEM((tm, d), BF16) for _ in range(1 + len(dilations))]
    if dilations:
        scratch.append(pltpu.VMEM((d // LANES, tm, LANES), F32))
    res = pl.pallas_call(
        functools.partial(_norm_matmul_kernel, outs=tuple(outs), has_rope=rope is not None,
                          dilations=dilations, n_cast=len(casts), x_parts=x_parts),
        out_shape=tuple(out_shapes + cast_shapes),
        grid=(t // tm, n // tn),
        in_specs=in_specs,
        out_specs=tuple(out_specs + cast_out),
        scratch_shapes=scratch,
        compiler_params=_params(("parallel", "arbitrary")),
        name="norm_matmul",
    )(*args)
    return res[:len(outs)], res[len(outs):]


def _retention_kernel(lg_ref, cd_ref, q_ref, k_ref, v_ref, g_ref, y_ref, r_ref, dm_ref, qd_ref, kd_ref, *, hps):
    hp = pl.program_id(1)
    n = pl.program_id(2)
    c = q_ref.shape[0]
    dk, dv = RET_QK_DIM, RET_V_DIM

    @pl.when(n == 0)
    def _():
        r_ref[...] = jnp.zeros_like(r_ref)
        i = lax.broadcasted_iota(jnp.int32, (c, c), 0)
        jj = lax.broadcasted_iota(jnp.int32, (c, c), 1)
        diff = i - jj
        row = lax.broadcasted_iota(jnp.int32, (c, LANES), 0).astype(F32)
        for hh in range(hps):
            lg = lg_ref[hp * hps + hh]
            dm_ref[hh] = jnp.where(diff >= 0, jnp.exp(lg * jnp.maximum(diff, 0).astype(F32)), 0.0)
            qd_ref[hh] = jnp.exp(lg * (row + 1.0))
            kd_ref[hh] = jnp.exp(lg * (F32(c - 1) - row))

    for hh in range(hps):
        q = q_ref[:, hh * dk:(hh + 1) * dk]
        k = k_ref[:, hh * dk:(hh + 1) * dk]
        v = v_ref[:, hh * dv:(hh + 1) * dv]
        s = lax.dot_general(q, k, (((1,), (1,)), ((), ())), preferred_element_type=F32) * dm_ref[hh]
        inner = jnp.dot(s.astype(BF16), v, preferred_element_type=F32)
        r_old = r_ref[hh]
        cross = jnp.dot(q, r_old.astype(BF16), preferred_element_type=F32)
        qd = qd_ref[hh]
        kd = kd_ref[hh]
        k_dec = jnp.concatenate(
            [(k[:, o:o + LANES].astype(F32) * kd).astype(BF16) for o in range(0, dk, LANES)], axis=1)
        r_ref[hh] = r_old * cd_ref[hp * hps + hh] + lax.dot_general(
            k_dec, v, (((0,), (0,)), ((), ())), preferred_element_type=F32)

        y = jnp.concatenate(
            [inner[:, o:o + LANES] + cross[:, o:o + LANES] * qd for o in range(0, dv, LANES)], axis=1)
        ms = jnp.mean(y * y, axis=-1, keepdims=True)
        gate = g_ref[:, hh * dv:(hh + 1) * dv].astype(F32)
        y_ref[:, hh * dv:(hh + 1) * dv] = ((y * lax.rsqrt(ms + NORM_EPS)) * gate).astype(y_ref.dtype)


def _retention(qk, v, gate, batch, seq, *, hps):
    t = qk.shape[0]
    c = RET_CHUNK
    nc = seq // c
    nhp = RET_HEADS // hps
    log_g = np.log(1.0 - 2.0 ** (-5.0 - np.arange(RET_HEADS))).astype(np.float32)
    c_dec = np.exp(log_g * c).astype(np.float32)
    smem = pl.BlockSpec(memory_space=pltpu.SMEM)
    qk_shape = (c, hps * RET_QK_DIM)
    v_shape = (c, hps * RET_V_DIM)
    return pl.pallas_call(
        functools.partial(_retention_kernel, hps=hps),
        out_shape=jax.ShapeDtypeStruct((t, RET_NV), BF16),
        grid=(batch, nhp, nc),
        in_specs=[
            smem, smem,
            pl.BlockSpec(qk_shape, lambda b, h, n: (b * nc + n, h)),
            pl.BlockSpec(qk_shape, lambda b, h, n: (b * nc + n, nhp + h)),
            pl.BlockSpec(v_shape, lambda b, h, n: (b * nc + n, h)),
            pl.BlockSpec(v_shape, lambda b, h, n: (b * nc + n, h)),
        ],
        out_specs=pl.BlockSpec(v_shape, lambda b, h, n: (b * nc + n, h)),
        scratch_shapes=[
            pltpu.VMEM((hps, RET_QK_DIM, RET_V_DIM), F32),
            pltpu.VMEM((hps, c, c), F32),
            pltpu.VMEM((hps, c, LANES), F32),
            pltpu.VMEM((hps, c, LANES), F32),
        ],
        compiler_params=_params(("parallel", "parallel", "arbitrary")),
        name="retention",
    )(jnp.asarray(log_g), jnp.asarray(c_dec), qk, qk, v, gate)


def _matmul_residual_kernel(*refs, n_cast):
    y_ref, w_ref, x_ref = refs[:3]
    cast_src = refs[3:3 + n_cast]
    o_ref = refs[3 + n_cast]
    cast_dst = refs[4 + n_cast:]
    o_ref[...] = x_ref[...] + jnp.dot(y_ref[...], w_ref[...], preferred_element_type=F32)
    _run_casts(cast_src, cast_dst)


def _matmul_residual(y, w, layer, x, *, tm, tn, casts=()):
    t, k = y.shape
    n = w.shape[2]
    cast_in, cast_out, cast_shapes = _cast_specs(casts, t // tm, n // tn, lambda j, i: i, lambda j, i: j)
    res = pl.pallas_call(
        functools.partial(_matmul_residual_kernel, n_cast=len(casts)),
        out_shape=(jax.ShapeDtypeStruct((t, n), F32), *cast_shapes),
        grid=(n // tn, t // tm),
        in_specs=[
            pl.BlockSpec((tm, k), lambda j, i: (i, 0)),
            pl.BlockSpec((None, k, tn), lambda j, i: (layer, 0, j)),
            pl.BlockSpec((tm, tn), lambda j, i: (i, j)),
            *cast_in,
        ],
        out_specs=(pl.BlockSpec((tm, tn), lambda j, i: (i, j)), *cast_out),
        compiler_params=_params(("parallel", "parallel")),
        name="matmul_residual",
    )(y, w, x, *[c[0] for c in casts])
    return res[0], res[1:]


def _ffn_kernel(*refs, final_norm):
    if final_norm:
        x_ref, g_ref, w1_ref, w2_ref, wo_ref, gf_ref, o_ref, h_ref = refs
    else:
        x_ref, g_ref, w1_ref, w2_ref, wo_ref, o_ref, h_ref = refs
    f = pl.program_id(1)

    @pl.when(f == 0)
    def _():
        _rms_rows([x_ref], g_ref, h_ref, copy_ref=o_ref)

    h = h_ref[...]
    z1 = jnp.dot(h, w1_ref[...], preferred_element_type=F32)
    z2 = jnp.dot(h, w2_ref[...], preferred_element_type=F32)
    a = (z1 * (1.0 / (1.0 + jnp.exp(-z1)))) * z2
    o_ref[...] += jnp.dot(a.astype(BF16), wo_ref[...], preferred_element_type=F32)

    if final_norm:
        @pl.when(f == pl.num_programs(1) - 1)
        def _():
            gf = gf_ref[...]

            def body(c, carry):
                r0 = pl.multiple_of(c * RMS_ROWS, RMS_ROWS)
                x = o_ref[pl.ds(r0, RMS_ROWS), :]
                ms = jnp.mean(x * x, axis=-1, keepdims=True)
                o_ref[pl.ds(r0, RMS_ROWS), :] = x * lax.rsqrt(ms + NORM_EPS) * gf
                return carry

            lax.fori_loop(0, o_ref.shape[0] // RMS_ROWS, body, 0)


def _ffn(x, gain, w_in, w_out, layer, *, tm, tf, final_gain=None):
    t, d = x.shape
    nf = FFN_DIM // tf
    in_specs = [
        pl.BlockSpec((tm, d), lambda i, f: (i, 0)),
        pl.BlockSpec((1, d), lambda i, f: (0, 0)),
        pl.BlockSpec((None, d, tf), lambda i, f: (layer, 0, f)),
        pl.BlockSpec((None, d, tf), lambda i, f: (layer, 0, nf + f)),
        pl.BlockSpec((None, tf, d), lambda i, f: (layer, f, 0)),
    ]
    args = [x, gain.reshape(1, d), w_in, w_in, w_out]
    if final_gain is not None:
        in_specs.append(pl.BlockSpec((1, d), lambda i, f: (0, 0)))
        args.append(final_gain.reshape(1, d))
    return pl.pallas_call(
        functools.partial(_ffn_kernel, final_norm=final_gain is not None),
        out_shape=jax.ShapeDtypeStruct((t, d), F32),
        grid=(t // tm, nf),
        in_specs=in_specs,
        out_specs=pl.BlockSpec((tm, d), lambda i, f: (i, 0)),
        scratch_shapes=[pltpu.VMEM((tm, d), BF16)],
        compiler_params=_params(("parallel", "arbitrary")),
        name="ffn",
    )(*args)


def _t5_bucket_table(dist):
    n = np.maximum(dist, 0)
    max_exact = REL_BUCKETS // 2
    large = max_exact + (np.log(np.maximum(n, 1) / max_exact) / np.log(REL_MAX_DIST / max_exact)
                         * (REL_BUCKETS - max_exact)).astype(np.int32)
    large = np.minimum(large, REL_BUCKETS - 1)
    return np.where(n < max_exact, n, large).astype(np.int32)


def _bias_kernel(tab_ref, bucket_ref, o_ref):
    col = pl.program_id(0) * ATT_HEADS + pl.program_id(1)
    bucket = bucket_ref[0]
    acc = jnp.zeros(bucket.shape, F32)
    for b in range(REL_BUCKETS):
        acc = jnp.where(bucket == b, tab_ref[b, col], acc)
    o_ref[0, 0] = acc * float(np.log2(np.e))


def _rel_bias_tiles(rel_bias):
    blk = ATT_BLK
    i = np.arange(blk)[:, None]
    c = np.arange(2 * blk)[None, :]
    delta = blk + i - c
    buckets = np.stack([_t5_bucket_table(np.maximum(delta, 0) * d) for _, d in DIL_CONFIGS])
    return pl.pallas_call(
        _bias_kernel,
        out_shape=jax.ShapeDtypeStruct((N_GROUPS, ATT_HEADS, blk, 2 * blk), F32),
        grid=(N_GROUPS, ATT_HEADS),
        in_specs=[
            pl.BlockSpec(memory_space=pltpu.SMEM),
            pl.BlockSpec((1, blk, 2 * blk), lambda g, h: (g, 0, 0)),
        ],
        out_specs=pl.BlockSpec((1, 1, blk, 2 * blk), lambda g, h: (g, h, 0, 0)),
        compiler_params=_params(("parallel", "parallel")),
        name="rel_bias",
    )(rel_bias, jnp.asarray(buckets))


def _att_kernel(q_ref, kp_ref, kc_ref, vp_ref, vc_ref, bias_ref, cap_ref, o_ref, lse_ref, *, d, hpb):
    n = pl.program_id(1)
    hb = pl.program_id(2)
    blk = ATT_BLK
    lane = lax.broadcasted_iota(jnp.int32, (blk, LANES), 1)
    log2e = float(np.log2(np.e))
    scale = ATT_HEAD_DIM ** -0.5 * log2e
    first = jnp.where(n > 0, 0, 1)

    def residue(r):
        rows = pl.ds(r, blk, stride=d) if d > 1 else slice(None)
        lse_tile = jnp.zeros((blk, LANES), F32)
        pair = []
        for hl in range(hpb):
            hs = slice(hl * ATT_HEAD_DIM, (hl + 1) * ATT_HEAD_DIM)
            q = q_ref[0, r, :, hs]
            kb = jnp.concatenate([kp_ref[0, r, :, hs], kc_ref[0, r, :, hs]], axis=0)
            vb = jnp.concatenate([vp_ref[0, r, :, hs], vc_ref[0, r, :, hs]], axis=0)
            s = lax.dot_general(q, kb, (((1,), (1,)), ((), ())), preferred_element_type=F32)
            s = jnp.minimum(s * scale + bias_ref[0, hl], cap_ref[first])
            m = jnp.max(s, axis=-1, keepdims=True)
            p = jnp.exp2(s - m)
            den = jnp.sum(p, axis=-1, keepdims=True)
            pair.append(jnp.dot(p.astype(BF16), vb, preferred_element_type=F32) / den)
            if len(pair) == 2:
                o_ref[hl // 2, rows, :] = pltpu.pack_elementwise(pair, packed_dtype=BF16)
                pair = []
            lse_tile = jnp.where(lane == hb * hpb + hl, (m + jnp.log2(den)) * (1.0 / log2e), lse_tile)
        lse_ref[0, rows, :] = lse_tile

    rpi = min(d, max(1, 2 * ATT_HEADS // hpb))

    def body(it, carry):
        for rr in range(rpi):
            residue(it * rpi + rr)
        return carry

    if d == rpi:
        body(0, 0)
    else:
        lax.fori_loop(0, d // rpi, body, 0)


def _band_caps():
    blk = ATT_BLK
    i = np.arange(blk)[:, None]
    c = np.arange(2 * blk)[None, :]
    delta = blk + i - c
    band = (delta >= 0) & (delta <= blk)
    caps = np.stack([band, band & (c >= blk)])
    return np.where(caps, np.finfo(np.float32).max, NEG_INF).astype(np.float32)


def _dilated_group(q, kv, bias, gi, batch, seq):
    d = DIL_CONFIGS[gi][1]
    blk = ATT_BLK
    nb = seq // d // blk
    hpb = min(ATT_HEADS, ATT_TILES_PER_STEP // d)
    nhb = ATT_HEADS // hpb
    band = (1, d, blk, hpb * ATT_HEAD_DIM)
    prev = lambda n: jnp.maximum(n - 1, 0)
    return pl.pallas_call(
        functools.partial(_att_kernel, d=d, hpb=hpb),
        out_shape=(jax.ShapeDtypeStruct((ATT_HEADS // 2, batch * seq, ATT_HEAD_DIM), jnp.int32),
                   jax.ShapeDtypeStruct((nhb, batch * seq, LANES), F32)),
        grid=(batch, nb, nhb),
        in_specs=[
            pl.BlockSpec(band, lambda b, n, hb: (b, 0, n, hb)),
            pl.BlockSpec(band, lambda b, n, hb: (b, 0, prev(n), hb)),
            pl.BlockSpec(band, lambda b, n, hb: (b, 0, n, hb)),
            pl.BlockSpec(band, lambda b, n, hb: (b, 0, prev(n), nhb + hb)),
            pl.BlockSpec(band, lambda b, n, hb: (b, 0, n, nhb + hb)),
            pl.BlockSpec((1, hpb, blk, 2 * blk), lambda b, n, hb: (gi, hb, 0, 0)),
            pl.BlockSpec((2, blk, 2 * blk), lambda b, n, hb: (0, 0, 0)),
        ],
        out_specs=(pl.BlockSpec((hpb // 2, d * blk, ATT_HEAD_DIM), lambda b, n, hb: (hb, b * nb + n, 0)),
                   pl.BlockSpec((1, d * blk, LANES), lambda b, n, hb: (hb, b * nb + n, 0))),
        compiler_params=_params(("parallel", "parallel", "parallel")),
        name="dilated_attention_g%d" % gi,
    )(q, kv, kv, kv, kv, bias, jnp.asarray(_band_caps()))


def _merge_out_kernel(o0_ref, o1_ref, o2_ref, l0_ref, l1_ref, l2_ref, x_ref, w_ref, out_ref, ma_ref, mb_ref):
    s = pl.program_id(0)

    def step(src_ref, dst_ref):
        out_ref[...] = x_ref[...] + jnp.dot(src_ref[...], w_ref[...], preferred_element_type=F32)
        l0 = jnp.sum(l0_ref[...], axis=0)
        l1 = jnp.sum(l1_ref[...], axis=0)
        l2 = jnp.sum(l2_ref[...], axis=0)
        mx = jnp.maximum(jnp.maximum(l0, l1), l2)
        e0 = jnp.exp(l0 - mx)
        e1 = jnp.exp(l1 - mx)
        e2 = jnp.exp(l2 - mx)
        den = e0 + e1 + e2
        w0 = e0 / den
        w1 = e1 / den
        w2 = e2 / den
        tm = l0.shape[0]
        for h in range(ATT_HEADS):
            hs = slice(h * ATT_HEAD_DIM, (h + 1) * ATT_HEAD_DIM)
            b0 = jnp.broadcast_to(w0[:, h:h + 1], (tm, ATT_HEAD_DIM))
            b1 = jnp.broadcast_to(w1[:, h:h + 1], (tm, ATT_HEAD_DIM))
            b2 = jnp.broadcast_to(w2[:, h:h + 1], (tm, ATT_HEAD_DIM))
            o0, o1, o2 = (
                pltpu.unpack_elementwise(o_ref[h // 2], index=h % 2, packed_dtype=BF16, unpacked_dtype=F32)
                for o_ref in (o0_ref, o1_ref, o2_ref))
            dst_ref[:, hs] = (b0 * o0 + b1 * o1 + b2 * o2).astype(dst_ref.dtype)

    @pl.when(s == 0)
    def _():
        mb_ref[...] = jnp.zeros_like(mb_ref)

    @pl.when(s % 2 == 0)
    def _():
        step(mb_ref, ma_ref)

    @pl.when(s % 2 == 1)
    def _():
        step(ma_ref, mb_ref)


def _merge_out(outs, lses, x, w, layer, *, tm):
    t, d = x.shape
    nt = t // tm
    cur = lambda s: jnp.minimum(s, nt - 1)
    prv = lambda s: jnp.maximum(s - 1, 0)
    heads = pl.BlockSpec((ATT_HEADS // 2, tm, ATT_HEAD_DIM), lambda s: (0, cur(s), 0))
    narrow = [pl.BlockSpec((l.shape[0], tm, LANES), lambda s: (0, cur(s), 0)) for l in lses]
    return pl.pallas_call(
        _merge_out_kernel,
        out_shape=jax.ShapeDtypeStruct((t, d), F32),
        grid=(nt + 1,),
        in_specs=[heads, heads, heads, *narrow,
                  pl.BlockSpec((tm, d), lambda s: (prv(s), 0)),
                  pl.BlockSpec((None, ATT_WIDTH, d), lambda s: (layer, 0, 0))],
        out_specs=pl.BlockSpec((tm, d), lambda s: (prv(s), 0)),
        scratch_shapes=[pltpu.VMEM((tm, ATT_WIDTH), BF16), pltpu.VMEM((tm, ATT_WIDTH), BF16)],
        compiler_params=_params(("arbitrary",)),
        name="merge_out",
    )(*outs, *lses, x, w)


def _as_residue_major(a, batch, seq):
    return a.reshape(batch, 1, seq, a.shape[1]) if a.ndim == 2 else a


def kernel(x, g_mix, g_ffn, w_ret_in, w_ret_out, g_kv, w_kv, w_att_q, w_att_out, rel_bias, w_ffn_in, w_ffn_out, g_final):
    batch, seq, d = x.shape
    t = batch * seq
    x = x.reshape(t, d)
    tn = 1024
    ret_dims = dict(batch=batch, seq=seq, tm=1024)
    kv_dims = dict(batch=batch, seq=seq, tm=1024, single_buffer_x=True)
    q_dims = dict(batch=batch, seq=seq, tm=512)

    inv = (1.0 / ROPE_BASE ** np.linspace(0.0, 1.0, ROPE_HALF)).astype(np.float32)
    ang = jnp.arange(seq, dtype=F32)[:, None] * jnp.asarray(inv)[None, :]
    rope = (jnp.cos(ang), jnp.sin(ang))

    qt, vt, wt = RET_NQ // tn, RET_NV // tn, ATT_WIDTH // tn
    ret_outs = (
        _Out(2 * RET_NQ, 1, "rope", (_Seg(0, qt, 0, 1.0), _Seg(qt, qt, qt, RET_QK_DIM ** -0.5))),
        _Out(RET_NV, 1, "cast", (_Seg(2 * qt, vt, 0, 1.0),)),
        _Out(RET_NV, 1, "silu", (_Seg(2 * qt + vt, vt, 0, 1.0),)),
    )
    kv_outs = tuple(
        _Out(2 * ATT_WIDTH, dd, "cast", (_Seg(gi * wt, wt, 0, 1.0), _Seg((N_GROUPS + gi) * wt, wt, wt, 1.0)))
        for gi, (_, dd) in enumerate(DIL_CONFIGS))
    q_outs = tuple(
        _Out(ATT_WIDTH, dd, "cast", (_Seg(gi * wt, wt, 0, 1.0),)) for gi, (_, dd) in enumerate(DIL_CONFIGS))

    def whole(w):
        return w.reshape((1, -1) + w.shape[-1:])

    ret_in = [_col_tiled(w_ret_in[0].astype(BF16), tn), None]
    ret_out = [w_ret_out[0].astype(BF16)[None], None]
    ffn_in = [None] * DEPTH
    ffn_out = [None] * DEPTH
    w_kv_b = att_q = att_out = None

    kvs = bias = None
    for l in range(DEPTH):
        if l == 0:
            (qk, v, gate), (ret_in[1], ret_out[1], ffn_in[0], ffn_out[0]) = _norm_matmul(
                x, g_mix[l], ret_in[l], 0, ret_outs, rope=rope, **ret_dims,
                casts=[(w_ret_in, 1, tn), (w_ret_out, 1, None), (w_ffn_in, 0, None), (w_ffn_out, 0, None)])
            y = _retention(qk, v, gate, batch, seq, hps=4)
            x, (ffn_in[2],) = _matmul_residual(y, ret_out[l], 0, x, tm=512, tn=1024, casts=[(w_ffn_in, 2, None)])
        elif l == 1:
            (qk, v, gate), (ffn_in[1], ffn_out[1], w_kv_b, att_q) = _norm_matmul(
                x, g_mix[l], ret_in[l], 0, ret_outs, rope=rope, **ret_dims,
                casts=[(w_ffn_in, 1, None), (w_ffn_out, 1, None), (whole(w_kv), 0, tn), (whole(w_att_q), 0, tn)])
            y = _retention(qk, v, gate, batch, seq, hps=4)
            x, (ffn_out[2], att_out) = _matmul_residual(
                y, ret_out[l], 0, x, tm=512, tn=1024, casts=[(w_ffn_out, 2, None), (whole(w_att_out), 0, None)])
            att_out = att_out.reshape(w_att_out.shape)
        else:
            j = l - N_RET_LAYERS
            if j == 0:
                kvs, (ffn_in[3], ffn_out[3]) = _norm_matmul(
                    x, g_kv, w_kv_b, 0, kv_outs, **kv_dims, casts=[(w_ffn_in, 3, None), (w_ffn_out, 3, None)])
                bias = _rel_bias_tiles(rel_bias)
            qs, _ = _norm_matmul(x, g_mix[l], att_q, j, q_outs, **q_dims)
            outs, lses = [], []
            for gi in range(N_GROUPS):
                o, lse = _dilated_group(_as_residue_major(qs[gi], batch, seq),
                                        _as_residue_major(kvs[gi], batch, seq), bias, gi, batch, seq)
                outs.append(o)
                lses.append(lse)
            x = _merge_out(outs, lses, x, att_out, j, tm=256)
        x = _ffn(x, g_ffn[l], ffn_in[l], ffn_out[l], 0, tm=1024, tf=512,
                 final_gain=g_final if l == DEPTH - 1 else None)
    return x.reshape(batch, seq, d)
```

```python
import functools
from typing import NamedTuple

import numpy as np
import jax
import jax.numpy as jnp
from jax import lax
from jax.experimental import pallas as pl
from jax.experimental.pallas import tpu as pltpu

F32 = jnp.float32
BF16 = jnp.bfloat16

D_MODEL = 2048
DEPTH = 4
N_RET_LAYERS = DEPTH // 2

RET_HEADS = 8
RET_QK_DIM = D_MODEL // RET_HEADS
RET_V_DIM = 2 * RET_QK_DIM
RET_NQ = RET_HEADS * RET_QK_DIM
RET_NV = RET_HEADS * RET_V_DIM
ROPE_BASE = 10000.0
ROPE_HALF = RET_QK_DIM // 2

DIL_CONFIGS = ((128, 1), (512, 4), (2048, 16))
N_GROUPS = len(DIL_CONFIGS)
ATT_HEAD_DIM = 128
ATT_HEADS = D_MODEL // ATT_HEAD_DIM
ATT_WIDTH = ATT_HEADS * ATT_HEAD_DIM
ATT_BLK = 128
ATT_TILES_PER_STEP = 64
REL_BUCKETS = 32
REL_MAX_DIST = 2048
FFN_DIM = -(-8 * D_MODEL // (3 * 256)) * 256
NORM_EPS = 1e-6
NEG_INF = -1e30

LANES = 128
RET_CHUNK = 256
RMS_ROWS = 128
X_PARTS = 4
VMEM_PHYSICAL = 64 * 1024 * 1024
VMEM_LIMIT = VMEM_PHYSICAL * 7 // 8


def _params(semantics):
    return pltpu.CompilerParams(dimension_semantics=semantics, vmem_limit_bytes=VMEM_LIMIT)


def _rms_rows(x_refs, g_ref, dst_ref, copy_ref=None, stage_ref=None):
    g = g_ref[...]
    part = x_refs[0].shape[0]

    for q, x_ref in enumerate(x_refs):
        def body(c, carry, q=q, x_ref=x_ref):
            r0 = pl.multiple_of(c * RMS_ROWS, RMS_ROWS)
            x = x_ref[pl.ds(r0, RMS_ROWS), :]
            ms = jnp.mean(x * x, axis=-1, keepdims=True)
            y = x * lax.rsqrt(ms + NORM_EPS) * g
            rows = pl.ds(q * part + r0, RMS_ROWS)
            dst_ref[rows, :] = y.astype(dst_ref.dtype)
            if copy_ref is not None:
                copy_ref[rows, :] = x
            if stage_ref is not None:
                for k in range(stage_ref.shape[0]):
                    stage_ref[k, rows, :] = y[:, k * LANES:(k + 1) * LANES]
            return carry

        lax.fori_loop(0, part // RMS_ROWS, body, 0)


def _cast_specs(casts, n_outer, n_inner, outer, inner):
    in_specs, out_specs, out_shapes = [], [], []
    for w, wl in casts:
        rows, cols = w.shape[1:]
        m = max(k for k in range(1, n_inner + 1) if cols % (k * LANES) == 0)
        blk = (None, rows // n_outer, cols // m)
        in_specs.append(pl.BlockSpec(
            blk, lambda *g, wl=wl, m=m: (wl, outer(*g), jnp.minimum(inner(*g), m - 1))))
        out_specs.append(pl.BlockSpec(
            blk, lambda *g, m=m: (0, outer(*g), jnp.minimum(inner(*g), m - 1))))
        out_shapes.append(jax.ShapeDtypeStruct((1, rows, cols), BF16))
    return in_specs, out_specs, out_shapes


def _run_casts(src_refs, dst_refs):
    for src_ref, dst_ref in zip(src_refs, dst_refs):
        dst_ref[...] = src_ref[...].astype(dst_ref.dtype)


class _Seg(NamedTuple):
    j0: int
    nt: int
    cb0: int
    scale: float


class _Out(NamedTuple):
    ncols: int
    dilation: int
    epilogue: str
    segs: tuple


def _col_block(out, j):
    cb = out.segs[0].cb0
    for seg in out.segs:
        cb = jnp.where(j >= seg.j0, seg.cb0 + jnp.minimum(j - seg.j0, seg.nt - 1), cb)
    return cb


def _norm_matmul_kernel(*refs, outs, has_rope, dilations, n_cast, x_parts):
    x_refs = refs[:x_parts]
    g_ref, w_ref = refs[x_parts:x_parts + 2]
    pos = x_parts + 2
    if has_rope:
        cos_ref, sin_ref = refs[pos:pos + 2]
        pos += 2
    cast_src = refs[pos:pos + n_cast]
    pos += n_cast
    out_refs = refs[pos:pos + len(outs)]
    pos += len(outs)
    cast_dst = refs[pos:pos + n_cast]
    pos += n_cast
    h_refs = {1: refs[pos]}
    for k, d in enumerate(dilations):
        h_refs[d] = refs[pos + 1 + k]
    stage_ref = refs[pos + 1 + len(dilations)] if dilations else None
    j = pl.program_id(1)
    tm = x_parts * x_refs[0].shape[0]

    @pl.when(j == 0)
    def _():
        _rms_rows(x_refs, g_ref, h_refs[1], stage_ref=stage_ref)
        for d in dilations:
            n_r = tm // d
            for r in range(d):
                for k in range(stage_ref.shape[0]):
                    h_refs[d][r * n_r:(r + 1) * n_r, k * LANES:(k + 1) * LANES] = (
                        stage_ref[k, pl.ds(r, n_r, stride=d), :].astype(BF16))

    def emit(out, o_ref):
        _run_casts(cast_src, cast_dst)
        acc = jnp.dot(h_refs[out.dilation][...], w_ref[...], preferred_element_type=F32)
        tn = acc.shape[1]
        if out.dilation > 1:
            n_r = tm // out.dilation
            for r in range(out.dilation):
                o_ref[0, r] = acc[r * n_r:(r + 1) * n_r, :].astype(o_ref.dtype)
        elif out.epilogue == "rope":
            scale = F32(out.segs[0].scale)
            for seg in out.segs[1:]:
                scale = jnp.where(j >= seg.j0, F32(seg.scale), scale)
            cos = cos_ref[...]
            sin = sin_ref[...]
            for c in range(0, tn, 2 * ROPE_HALF):
                t1 = acc[:, c:c + ROPE_HALF] * scale
                t2 = acc[:, c + ROPE_HALF:c + 2 * ROPE_HALF] * scale
                o_ref[:, c:c + ROPE_HALF] = (t1 * cos - t2 * sin).astype(o_ref.dtype)
                o_ref[:, c + ROPE_HALF:c + 2 * ROPE_HALF] = (t2 * cos + t1 * sin).astype(o_ref.dtype)
        elif out.epilogue == "silu":
            o_ref[...] = (acc * (1.0 / (1.0 + jnp.exp(-acc)))).astype(o_ref.dtype)
        else:
            o_ref[...] = acc.astype(o_ref.dtype)

    for out, o_ref in zip(outs, out_refs):
        hit = None
        for seg in out.segs:
            inside = (j >= seg.j0) & (j < seg.j0 + seg.nt)
            hit = inside if hit is None else hit | inside
        pl.when(hit)(functools.partial(emit, out, o_ref))


def _norm_matmul(x, gain, w, layer, outs, *, batch, seq, tm, tn, rope=None, single_buffer_x=False, casts=()):
    t, d = x.shape
    n = w.shape[2]
    tiles_per_batch = seq // tm
    dilations = tuple(sorted({o.dilation for o in outs} - {1}))
    ni, nj = t // tm, n // tn
    if single_buffer_x:
        x_parts = 1
        x_specs = [pl.BlockSpec((tm, d), lambda i, j: (i, 0), pipeline_mode=pl.Buffered(1))]
    else:
        x_parts = min(X_PARTS, nj - 1)
        x_specs = [
            pl.BlockSpec((tm // x_parts, d),
                         lambda i, j, q=q: (x_parts * jnp.minimum(i + (j >= nj - x_parts + q), ni - 1) + q, 0))
            for q in range(x_parts)]
    in_specs = x_specs + [
        pl.BlockSpec((1, d), lambda i, j: (0, 0)),
        pl.BlockSpec((None, d, tn), lambda i, j: (layer, 0, j)),
    ]
    args = [x] * x_parts + [gain.reshape(1, d), w]
    if rope is not None:
        in_specs += [pl.BlockSpec((tm, ROPE_HALF), lambda i, j: (i % tiles_per_batch, 0))] * 2
        args += list(rope)
    cast_in, cast_out, cast_shapes = _cast_specs(casts, t // tm, n // tn, lambda i, j: i, lambda i, j: j)
    in_specs += cast_in
    args += [w_ for w_, _ in casts]
    out_shapes, out_specs = [], []
    for out in outs:
        dd = out.dilation
        if dd == 1:
            out_shapes.append(jax.ShapeDtypeStruct((t, out.ncols), BF16))
            out_specs.append(pl.BlockSpec((tm, tn), lambda i, j, out=out: (i, _col_block(out, j))))
        else:
            out_shapes.append(jax.ShapeDtypeStruct((batch, dd, seq // dd, out.ncols), BF16))
            out_specs.append(pl.BlockSpec(
                (1, dd, tm // dd, tn),
                lambda i, j, out=out: (i // tiles_per_batch, 0, i % tiles_per_batch, _col_block(out, j))))
    scratch = [pltpu.VMEM((tm, d), BF16) for _ in range(1 + len(dilations))]
    if dilations:
        scratch.append(pltpu.VMEM((d // LANES, tm, LANES), F32))
    res = pl.pallas_call(
        functools.partial(_norm_matmul_kernel, outs=tuple(outs), has_rope=rope is not None,
                          dilations=dilations, n_cast=len(casts), x_parts=x_parts),
        out_shape=tuple(out_shapes + cast_shapes),
        grid=(t // tm, n // tn),
        in_specs=in_specs,
        out_specs=tuple(out_specs + cast_out),
        scratch_shapes=scratch,
        compiler_params=_params(("parallel", "arbitrary")),
        name="norm_matmul",
    )(*args)
    return res[:len(outs)], res[len(outs):]


def _retention_kernel(lg_ref, cd_ref, q_ref, k_ref, v_ref, g_ref, y_ref, r_ref, dm_ref, qd_ref, kd_ref, *, hps):
    hp = pl.program_id(1)
    n = pl.program_id(2)
    c = q_ref.shape[0]
    dk, dv = RET_QK_DIM, RET_V_DIM

    @pl.when(n == 0)
    def _():
        r_ref[...] = jnp.zeros_like(r_ref)
        i = lax.broadcasted_iota(jnp.int32, (c, c), 0)
        jj = lax.broadcasted_iota(jnp.int32, (c, c), 1)
        diff = i - jj
        row = lax.broadcasted_iota(jnp.int32, (c, LANES), 0).astype(F32)
        for hh in range(hps):
            lg = lg_ref[hp * hps + hh]
            dm_ref[hh] = jnp.where(diff >= 0, jnp.exp(lg * jnp.maximum(diff, 0).astype(F32)), 0.0)
            qd_ref[hh] = jnp.exp(lg * (row + 1.0))
            kd_ref[hh] = jnp.exp(lg * (F32(c - 1) - row))

    for hh in range(hps):
        q = q_ref[:, hh * dk:(hh + 1) * dk]
        k = k_ref[:, hh * dk:(hh + 1) * dk]
        v = v_ref[:, hh * dv:(hh + 1) * dv]
        s = lax.dot_general(q, k, (((1,), (1,)), ((), ())), preferred_element_type=F32) * dm_ref[hh]
        inner = jnp.dot(s.astype(BF16), v, preferred_element_type=F32)
        r_old = r_ref[hh]
        cross = jnp.dot(q, r_old.astype(BF16), preferred_element_type=F32)
        qd = qd_ref[hh]
        kd = kd_ref[hh]
        k_dec = jnp.concatenate(
            [(k[:, o:o + LANES].astype(F32) * kd).astype(BF16) for o in range(0, dk, LANES)], axis=1)
        r_ref[hh] = r_old * cd_ref[hp * hps + hh] + lax.dot_general(
            k_dec, v, (((0,), (0,)), ((), ())), preferred_element_type=F32)

        y = jnp.concatenate(
            [inner[:, o:o + LANES] + cross[:, o:o + LANES] * qd for o in range(0, dv, LANES)], axis=1)
        ms = jnp.mean(y * y, axis=-1, keepdims=True)
        gate = g_ref[:, hh * dv:(hh + 1) * dv].astype(F32)
        y_ref[:, hh * dv:(hh + 1) * dv] = ((y * lax.rsqrt(ms + NORM_EPS)) * gate).astype(y_ref.dtype)


def _retention(qk, v, gate, batch, seq, *, hps):
    t = qk.shape[0]
    c = RET_CHUNK
    nc = seq // c
    nhp = RET_HEADS // hps
    log_g = np.log(1.0 - 2.0 ** (-5.0 - np.arange(RET_HEADS))).astype(np.float32)
    c_dec = np.exp(log_g * c).astype(np.float32)
    smem = pl.BlockSpec(memory_space=pltpu.SMEM)
    qk_shape = (c, hps * RET_QK_DIM)
    v_shape = (c, hps * RET_V_DIM)
    return pl.pallas_call(
        functools.partial(_retention_kernel, hps=hps),
        out_shape=jax.ShapeDtypeStruct((t, RET_NV), BF16),
        grid=(batch, nhp, nc),
        in_specs=[
            smem, smem,
            pl.BlockSpec(qk_shape, lambda b, h, n: (b * nc + n, h)),
            pl.BlockSpec(qk_shape, lambda b, h, n: (b * nc + n, nhp + h)),
            pl.BlockSpec(v_shape, lambda b, h, n: (b * nc + n, h)),
            pl.BlockSpec(v_shape, lambda b, h, n: (b * nc + n, h)),
        ],
        out_specs=pl.BlockSpec(v_shape, lambda b, h, n: (b * nc + n, h)),
        scratch_shapes=[
            pltpu.VMEM((hps, RET_QK_DIM, RET_V_DIM), F32),
            pltpu.VMEM((hps, c, c), F32),
            pltpu.VMEM((hps, c, LANES), F32),
            pltpu.VMEM((hps, c, LANES), F32),
        ],
        compiler_params=_params(("parallel", "parallel", "arbitrary")),
        name="retention",
    )(jnp.asarray(log_g), jnp.asarray(c_dec), qk, qk, v, gate)


def _matmul_residual_kernel(*refs, n_cast):
    y_ref, w_ref, x_ref = refs[:3]
    cast_src = refs[3:3 + n_cast]
    o_ref = refs[3 + n_cast]
    cast_dst = refs[4 + n_cast:]
    o_ref[...] = x_ref[...] + jnp.dot(y_ref[...], w_ref[...], preferred_element_type=F32)
    _run_casts(cast_src, cast_dst)


def _matmul_residual(y, w, layer, x, *, tm, tn, casts=()):
    t, k = y.shape
    n = w.shape[2]
    cast_in, cast_out, cast_shapes = _cast_specs(casts, t // tm, n // tn, lambda j, i: i, lambda j, i: j)
    res = pl.pallas_call(
        functools.partial(_matmul_residual_kernel, n_cast=len(casts)),
        out_shape=(jax.ShapeDtypeStruct((t, n), F32), *cast_shapes),
        grid=(n // tn, t // tm),
        in_specs=[
            pl.BlockSpec((tm, k), lambda j, i: (i, 0)),
            pl.BlockSpec((None, k, tn), lambda j, i: (layer, 0, j)),
            pl.BlockSpec((tm, tn), lambda j, i: (i, j)),
            *cast_in,
        ],
        out_specs=(pl.BlockSpec((tm, tn), lambda j, i: (i, j)), *cast_out),
        compiler_params=_params(("parallel", "parallel")),
        name="matmul_residual",
    )(y, w, x, *[w_ for w_, _ in casts])
    return res[0], res[1:]


def _ffn_kernel(*refs, final_norm):
    if final_norm:
        x_ref, g_ref, w1_ref, w2_ref, wo_ref, gf_ref, o_ref, h_ref = refs
    else:
        x_ref, g_ref, w1_ref, w2_ref, wo_ref, o_ref, h_ref = refs
    f = pl.program_id(1)

    @pl.when(f == 0)
    def _():
        _rms_rows([x_ref], g_ref, h_ref, copy_ref=o_ref)

    h = h_ref[...]
    z1 = jnp.dot(h, w1_ref[...], preferred_element_type=F32)
    z2 = jnp.dot(h, w2_ref[...], preferred_element_type=F32)
    a = (z1 * (1.0 / (1.0 + jnp.exp(-z1)))) * z2
    o_ref[...] += jnp.dot(a.astype(BF16), wo_ref[...], preferred_element_type=F32)

    if final_norm:
        @pl.when(f == pl.num_programs(1) - 1)
        def _():
            gf = gf_ref[...]

            def body(c, carry):
                r0 = pl.multiple_of(c * RMS_ROWS, RMS_ROWS)
                x = o_ref[pl.ds(r0, RMS_ROWS), :]
                ms = jnp.mean(x * x, axis=-1, keepdims=True)
                o_ref[pl.ds(r0, RMS_ROWS), :] = x * lax.rsqrt(ms + NORM_EPS) * gf
                return carry

            lax.fori_loop(0, o_ref.shape[0] // RMS_ROWS, body, 0)


def _ffn(x, gain, w_in, w_out, layer, *, tm, tf, final_gain=None):
    t, d = x.shape
    nf = FFN_DIM // tf
    in_specs = [
        pl.BlockSpec((tm, d), lambda i, f: (i, 0)),
        pl.BlockSpec((1, d), lambda i, f: (0, 0)),
        pl.BlockSpec((None, d, tf), lambda i, f: (layer, 0, f)),
        pl.BlockSpec((None, d, tf), lambda i, f: (layer, 0, nf + f)),
        pl.BlockSpec((None, tf, d), lambda i, f: (layer, f, 0)),
    ]
    args = [x, gain.reshape(1, d), w_in, w_in, w_out]
    if final_gain is not None:
        in_specs.append(pl.BlockSpec((1, d), lambda i, f: (0, 0)))
        args.append(final_gain.reshape(1, d))
    return pl.pallas_call(
        functools.partial(_ffn_kernel, final_norm=final_gain is not None),
        out_shape=jax.ShapeDtypeStruct((t, d), F32),
        grid=(t // tm, nf),
        in_specs=in_specs,
        out_specs=pl.BlockSpec((tm, d), lambda i, f: (i, 0)),
        scratch_shapes=[pltpu.VMEM((tm, d), BF16)],
        compiler_params=_params(("parallel", "arbitrary")),
        name="ffn",
    )(*args)


def _t5_bucket_table(dist):
    n = np.maximum(dist, 0)
    max_exact = REL_BUCKETS // 2
    large = max_exact + (np.log(np.maximum(n, 1) / max_exact) / np.log(REL_MAX_DIST / max_exact)
                         * (REL_BUCKETS - max_exact)).astype(np.int32)
    large = np.minimum(large, REL_BUCKETS - 1)
    return np.where(n < max_exact, n, large).astype(np.int32)


def _bias_kernel(tab_ref, bucket_ref, o_ref):
    col = pl.program_id(0) * ATT_HEADS + pl.program_id(1)
    bucket = bucket_ref[0]
    acc = jnp.zeros(bucket.shape, F32)
    for b in range(REL_BUCKETS):
        acc = jnp.where(bucket == b, tab_ref[b, col], acc)
    o_ref[0, 0] = acc * float(np.log2(np.e))


def _rel_bias_tiles(rel_bias):
    blk = ATT_BLK
    i = np.arange(blk)[:, None]
    c = np.arange(2 * blk)[None, :]
    delta = blk + i - c
    buckets = np.stack([_t5_bucket_table(np.maximum(delta, 0) * d) for _, d in DIL_CONFIGS])
    return pl.pallas_call(
        _bias_kernel,
        out_shape=jax.ShapeDtypeStruct((N_GROUPS, ATT_HEADS, blk, 2 * blk), F32),
        grid=(N_GROUPS, ATT_HEADS),
        in_specs=[
            pl.BlockSpec(memory_space=pltpu.SMEM),
            pl.BlockSpec((1, blk, 2 * blk), lambda g, h: (g, 0, 0)),
        ],
        out_specs=pl.BlockSpec((1, 1, blk, 2 * blk), lambda g, h: (g, h, 0, 0)),
        compiler_params=_params(("parallel", "parallel")),
        name="rel_bias",
    )(rel_bias, jnp.asarray(buckets))


def _att_kernel(*refs, d, hpb, fused_kv):
    if fused_kv:
        q_ref, kp_ref, kc_ref, bias_ref, cap_ref, o_ref, lse_ref = refs
        vp_ref, vc_ref, v_off = kp_ref, kc_ref, ATT_WIDTH
    else:
        q_ref, kp_ref, kc_ref, vp_ref, vc_ref, bias_ref, cap_ref, o_ref, lse_ref = refs
        v_off = 0
    n = pl.program_id(1)
    hb = pl.program_id(2)
    blk = ATT_BLK
    lane = lax.broadcasted_iota(jnp.int32, (blk, LANES), 1)
    log2e = float(np.log2(np.e))
    scale = ATT_HEAD_DIM ** -0.5 * log2e
    first = jnp.where(n > 0, 0, 1)

    def residue(r):
        rows = pl.ds(r, blk, stride=d) if d > 1 else slice(None)
        lse_tile = jnp.zeros((blk, LANES), F32)
        pair = []
        for hl in range(hpb):
            hs = slice(hl * ATT_HEAD_DIM, (hl + 1) * ATT_HEAD_DIM)
            q = q_ref[0, r, :, hs]
            kb = jnp.concatenate([kp_ref[0, r, :, hs], kc_ref[0, r, :, hs]], axis=0)
            vs = slice(v_off + hs.start, v_off + hs.stop)
            vb = jnp.concatenate([vp_ref[0, r, :, vs], vc_ref[0, r, :, vs]], axis=0)
            s = lax.dot_general(q, kb, (((1,), (1,)), ((), ())), preferred_element_type=F32)
            s = jnp.minimum(s * scale + bias_ref[0, hl], cap_ref[first])
            m = jnp.max(s, axis=-1, keepdims=True)
            p = jnp.exp2(s - m)
            den = jnp.sum(p, axis=-1, keepdims=True)
            pair.append(jnp.dot(p.astype(BF16), vb, preferred_element_type=F32) / den)
            if len(pair) == 2:
                o_ref[hl // 2, rows, :] = pltpu.pack_elementwise(pair, packed_dtype=BF16)
                pair = []
            lse_tile = jnp.where(lane == hb * hpb + hl, (m + jnp.log2(den)) * (1.0 / log2e), lse_tile)
        lse_ref[0, rows, :] = lse_tile

    rpi = min(d, max(1, 2 * ATT_HEADS // hpb))

    def body(it, carry):
        for rr in range(rpi):
            residue(it * rpi + rr)
        return carry

    if d == rpi:
        body(0, 0)
    else:
        lax.fori_loop(0, d // rpi, body, 0)


def _band_caps():
    blk = ATT_BLK
    i = np.arange(blk)[:, None]
    c = np.arange(2 * blk)[None, :]
    delta = blk + i - c
    band = (delta >= 0) & (delta <= blk)
    caps = np.stack([band, band & (c >= blk)])
    return np.where(caps, np.finfo(np.float32).max, NEG_INF).astype(np.float32)


def _dilated_group(q, kv, bias, gi, batch, seq):
    d = DIL_CONFIGS[gi][1]
    blk = ATT_BLK
    nb = seq // d // blk
    hpb = min(ATT_HEADS, ATT_TILES_PER_STEP // d)
    nhb = ATT_HEADS // hpb
    band = (1, d, blk, hpb * ATT_HEAD_DIM)
    prev = lambda n: jnp.maximum(n - 1, 0)
    fused_kv = nhb == 1
    if fused_kv:
        kv_band = (1, d, blk, 2 * ATT_WIDTH)
        kv_specs = [pl.BlockSpec(kv_band, lambda b, n, hb: (b, 0, prev(n), 0)),
                    pl.BlockSpec(kv_band, lambda b, n, hb: (b, 0, n, 0))]
    else:
        kv_specs = [pl.BlockSpec(band, lambda b, n, hb: (b, 0, prev(n), hb)),
                    pl.BlockSpec(band, lambda b, n, hb: (b, 0, n, hb)),
                    pl.BlockSpec(band, lambda b, n, hb: (b, 0, prev(n), nhb + hb)),
                    pl.BlockSpec(band, lambda b, n, hb: (b, 0, n, nhb + hb))]
    return pl.pallas_call(
        functools.partial(_att_kernel, d=d, hpb=hpb, fused_kv=fused_kv),
        out_shape=(jax.ShapeDtypeStruct((ATT_HEADS // 2, batch * seq, ATT_HEAD_DIM), jnp.int32),
                   jax.ShapeDtypeStruct((nhb, batch * seq, LANES), F32)),
        grid=(batch, nb, nhb),
        in_specs=[
            pl.BlockSpec(band, lambda b, n, hb: (b, 0, n, hb)),
            *kv_specs,
            pl.BlockSpec((1, hpb, blk, 2 * blk), lambda b, n, hb: (gi, hb, 0, 0)),
            pl.BlockSpec((2, blk, 2 * blk), lambda b, n, hb: (0, 0, 0)),
        ],
        out_specs=(pl.BlockSpec((hpb // 2, d * blk, ATT_HEAD_DIM), lambda b, n, hb: (hb, b * nb + n, 0)),
                   pl.BlockSpec((1, d * blk, LANES), lambda b, n, hb: (hb, b * nb + n, 0))),
        compiler_params=_params(("parallel", "parallel", "parallel")),
        name="dilated_attention_g%d" % gi,
    )(q, *[kv] * len(kv_specs), bias, jnp.asarray(_band_caps()))


def _merge_out_kernel(o0_ref, o1_ref, o2_ref, l0_ref, l1_ref, l2_ref, x_ref, w_ref, out_ref, ma_ref, mb_ref):
    s = pl.program_id(0)

    def step(src_ref, dst_ref):
        out_ref[...] = x_ref[...] + jnp.dot(src_ref[...], w_ref[...], preferred_element_type=F32)
        l0 = jnp.sum(l0_ref[...], axis=0)
        l1 = jnp.sum(l1_ref[...], axis=0)
        l2 = jnp.sum(l2_ref[...], axis=0)
        mx = jnp.maximum(jnp.maximum(l0, l1), l2)
        e0 = jnp.exp(l0 - mx)
        e1 = jnp.exp(l1 - mx)
        e2 = jnp.exp(l2 - mx)
        den = e0 + e1 + e2
        w0 = e0 / den
        w1 = e1 / den
        w2 = e2 / den
        tm = l0.shape[0]
        for h in range(ATT_HEADS):
            hs = slice(h * ATT_HEAD_DIM, (h + 1) * ATT_HEAD_DIM)
            b0 = jnp.broadcast_to(w0[:, h:h + 1], (tm, ATT_HEAD_DIM))
            b1 = jnp.broadcast_to(w1[:, h:h + 1], (tm, ATT_HEAD_DIM))
            b2 = jnp.broadcast_to(w2[:, h:h + 1], (tm, ATT_HEAD_DIM))
            o0, o1, o2 = (
                pltpu.unpack_elementwise(o_ref[h // 2], index=h % 2, packed_dtype=BF16, unpacked_dtype=F32)
                for o_ref in (o0_ref, o1_ref, o2_ref))
            dst_ref[:, hs] = (b0 * o0 + b1 * o1 + b2 * o2).astype(dst_ref.dtype)

    @pl.when(s == 0)
    def _():
        mb_ref[...] = jnp.zeros_like(mb_ref)

    @pl.when(s % 2 == 0)
    def _():
        step(mb_ref, ma_ref)

    @pl.when(s % 2 == 1)
    def _():
        step(ma_ref, mb_ref)


def _merge_out(outs, lses, x, w, layer, *, tm):
    t, d = x.shape
    nt = t // tm
    cur = lambda s: jnp.minimum(s, nt - 1)
    prv = lambda s: jnp.maximum(s - 1, 0)
    heads = pl.BlockSpec((ATT_HEADS // 2, tm, ATT_HEAD_DIM), lambda s: (0, cur(s), 0))
    narrow = [pl.BlockSpec((l.shape[0], tm, LANES), lambda s: (0, cur(s), 0)) for l in lses]
    return pl.pallas_call(
        _merge_out_kernel,
        out_shape=jax.ShapeDtypeStruct((t, d), F32),
        grid=(nt + 1,),
        in_specs=[heads, heads, heads, *narrow,
                  pl.BlockSpec((tm, d), lambda s: (prv(s), 0)),
                  pl.BlockSpec((None, ATT_WIDTH, d), lambda s: (layer, 0, 0))],
        out_specs=pl.BlockSpec((tm, d), lambda s: (prv(s), 0)),
        scratch_shapes=[pltpu.VMEM((tm, ATT_WIDTH), BF16), pltpu.VMEM((tm, ATT_WIDTH), BF16)],
        compiler_params=_params(("arbitrary",)),
        name="merge_out",
    )(*outs, *lses, x, w)


def _as_residue_major(a, batch, seq):
    return a.reshape(batch, 1, seq, a.shape[1]) if a.ndim == 2 else a


def kernel(x, g_mix, g_ffn, w_ret_in, w_ret_out, g_kv, w_kv, w_att_q, w_att_out, rel_bias, w_ffn_in, w_ffn_out, g_final):
    batch, seq, d = x.shape
    t = batch * seq
    x = x.reshape(t, d)
    tn = 1024
    ret_dims = dict(batch=batch, seq=seq, tm=1024, tn=tn)
    kv_dims = dict(batch=batch, seq=seq, tm=1024, tn=tn, single_buffer_x=True)
    q_dims = dict(batch=batch, seq=seq, tm=512, tn=tn)

    inv = (1.0 / ROPE_BASE ** np.linspace(0.0, 1.0, ROPE_HALF)).astype(np.float32)
    ang = jnp.arange(seq, dtype=F32)[:, None] * jnp.asarray(inv)[None, :]
    rope = (jnp.cos(ang), jnp.sin(ang))

    qt, vt, wt = RET_NQ // tn, RET_NV // tn, ATT_WIDTH // tn
    ret_outs = (
        _Out(2 * RET_NQ, 1, "rope", (_Seg(0, qt, 0, 1.0), _Seg(qt, qt, qt, RET_QK_DIM ** -0.5))),
        _Out(RET_NV, 1, "cast", (_Seg(2 * qt, vt, 0, 1.0),)),
        _Out(RET_NV, 1, "silu", (_Seg(2 * qt + vt, vt, 0, 1.0),)),
    )
    kv_outs = tuple(
        _Out(2 * ATT_WIDTH, dd, "cast", (_Seg(gi * wt, wt, 0, 1.0), _Seg((N_GROUPS + gi) * wt, wt, wt, 1.0)))
        for gi, (_, dd) in enumerate(DIL_CONFIGS))
    q_outs = tuple(
        _Out(ATT_WIDTH, dd, "cast", (_Seg(gi * wt, wt, 0, 1.0),)) for gi, (_, dd) in enumerate(DIL_CONFIGS))

    def whole(w):
        return w.reshape((1, -1) + w.shape[-1:])

    ret_in = [w_ret_in[0].astype(BF16)[None], None]
    ret_out = [w_ret_out[0].astype(BF16)[None], None]
    ffn_in = [None] * DEPTH
    ffn_out = [None] * DEPTH
    w_kv_b = att_q = att_out = None

    kvs = bias = None
    for l in range(DEPTH):
        if l == 0:
            (qk, v, gate), (ret_in[1], ret_out[1], ffn_in[0], ffn_out[0]) = _norm_matmul(
                x, g_mix[l], ret_in[l], 0, ret_outs, rope=rope, **ret_dims,
                casts=[(w_ret_in, 1), (w_ret_out, 1), (w_ffn_in, 0), (w_ffn_out, 0)])
            y = _retention(qk, v, gate, batch, seq, hps=4)
            x, (ffn_in[2],) = _matmul_residual(y, ret_out[l], 0, x, tm=512, tn=1024, casts=[(w_ffn_in, 2)])
        elif l == 1:
            (qk, v, gate), (ffn_in[1], ffn_out[1], w_kv_b, att_q) = _norm_matmul(
                x, g_mix[l], ret_in[l], 0, ret_outs, rope=rope, **ret_dims,
                casts=[(w_ffn_in, 1), (w_ffn_out, 1), (whole(w_kv), 0), (whole(w_att_q), 0)])
            att_q = att_q.reshape(w_att_q.shape)
            y = _retention(qk, v, gate, batch, seq, hps=4)
            x, (ffn_out[2], att_out) = _matmul_residual(
                y, ret_out[l], 0, x, tm=512, tn=1024, casts=[(w_ffn_out, 2), (whole(w_att_out), 0)])
            att_out = att_out.reshape(w_att_out.shape)
        else:
            j = l - N_RET_LAYERS
            if j == 0:
                kvs, (ffn_in[3], ffn_out[3]) = _norm_matmul(
                    x, g_kv, w_kv_b, 0, kv_outs, **kv_dims, casts=[(w_ffn_in, 3), (w_ffn_out, 3)])
                bias = _rel_bias_tiles(rel_bias)
            qs, _ = _norm_matmul(x, g_mix[l], att_q, j, q_outs, **q_dims)
            outs, lses = [], []
            for gi in range(N_GROUPS):
                o, lse = _dilated_group(_as_residue_major(qs[gi], batch, seq),
                                        _as_residue_major(kvs[gi], batch, seq), bias, gi, batch, seq)
                outs.append(o)
                lses.append(lse)
            x = _merge_out(outs, lses, x, att_out, j, tm=512)
        x = _ffn(x, g_ffn[l], ffn_in[l], ffn_out[l], 0, tm=1024, tf=512,
                 final_gain=g_final if l == DEPTH - 1 else None)
    return x.reshape(batch, seq, d)
```

```python
import functools
from typing import NamedTuple

import numpy as np
import jax
import jax.numpy as jnp
from jax import lax
from jax.experimental import pallas as pl
from jax.experimental.pallas import tpu as pltpu

F32 = jnp.float32
BF16 = jnp.bfloat16

D_MODEL = 2048
DEPTH = 4
N_RET_LAYERS = DEPTH // 2

RET_HEADS = 8
RET_QK_DIM = D_MODEL // RET_HEADS
RET_V_DIM = 2 * RET_QK_DIM
RET_NQ = RET_HEADS * RET_QK_DIM
RET_NV = RET_HEADS * RET_V_DIM
ROPE_BASE = 10000.0
ROPE_HALF = RET_QK_DIM // 2

DIL_CONFIGS = ((128, 1), (512, 4), (2048, 16))
N_GROUPS = len(DIL_CONFIGS)
ATT_HEAD_DIM = 128
ATT_HEADS = D_MODEL // ATT_HEAD_DIM
ATT_WIDTH = ATT_HEADS * ATT_HEAD_DIM
ATT_BLK = 128
ATT_TILES_PER_STEP = 64
REL_BUCKETS = 32
REL_MAX_DIST = 2048
FFN_DIM = -(-8 * D_MODEL // (3 * 256)) * 256
NORM_EPS = 1e-6
NEG_INF = -1e30

LANES = 128
RET_CHUNK = 256
RMS_ROWS = 128
X_PARTS = 4
VMEM_PHYSICAL = 64 * 1024 * 1024
VMEM_LIMIT = VMEM_PHYSICAL * 7 // 8


def _params(semantics):
    return pltpu.CompilerParams(dimension_semantics=semantics, vmem_limit_bytes=VMEM_LIMIT)


def _rms_rows(x_refs, g_ref, dst_ref, copy_ref=None, stage_ref=None):
    g = g_ref[...]
    part = x_refs[0].shape[0]

    for q, x_ref in enumerate(x_refs):
        def body(c, carry, q=q, x_ref=x_ref):
            r0 = pl.multiple_of(c * RMS_ROWS, RMS_ROWS)
            x = x_ref[pl.ds(r0, RMS_ROWS), :]
            ms = jnp.mean(x * x, axis=-1, keepdims=True)
            y = x * lax.rsqrt(ms + NORM_EPS) * g
            rows = pl.ds(q * part + r0, RMS_ROWS)
            dst_ref[rows, :] = y.astype(dst_ref.dtype)
            if copy_ref is not None:
                copy_ref[rows, :] = x
            if stage_ref is not None:
                for k in range(stage_ref.shape[0]):
                    stage_ref[k, rows, :] = y[:, k * LANES:(k + 1) * LANES]
            return carry

        lax.fori_loop(0, part // RMS_ROWS, body, 0)


def _cast_specs(casts, n_outer, n_inner, outer, inner):
    in_specs, out_specs, out_shapes = [], [], []
    for w, wl in casts:
        rows, cols = w.shape[1:]
        m = max(k for k in range(1, n_inner + 1) if cols % (k * LANES) == 0)
        blk = (None, rows // n_outer, cols // m)
        in_specs.append(pl.BlockSpec(
            blk, lambda *g, wl=wl, m=m: (wl, outer(*g), jnp.minimum(inner(*g), m - 1))))
        out_specs.append(pl.BlockSpec(
            blk, lambda *g, m=m: (0, outer(*g), jnp.minimum(inner(*g), m - 1))))
        out_shapes.append(jax.ShapeDtypeStruct((1, rows, cols), BF16))
    return in_specs, out_specs, out_shapes


def _run_casts(src_refs, dst_refs):
    for src_ref, dst_ref in zip(src_refs, dst_refs):
        dst_ref[...] = src_ref[...].astype(dst_ref.dtype)


class _Seg(NamedTuple):
    j0: int
    nt: int
    cb0: int
    scale: float


class _Out(NamedTuple):
    ncols: int
    dilation: int
    epilogue: str
    segs: tuple


def _col_block(out, j):
    cb = out.segs[0].cb0
    for seg in out.segs:
        cb = jnp.where(j >= seg.j0, seg.cb0 + jnp.minimum(j - seg.j0, seg.nt - 1), cb)
    return cb


def _norm_matmul_kernel(*refs, outs, has_rope, dilations, n_cast, x_parts):
    x_refs = refs[:x_parts]
    g_ref, w_ref = refs[x_parts:x_parts + 2]
    pos = x_parts + 2
    if has_rope:
        cos_ref, sin_ref = refs[pos:pos + 2]
        pos += 2
    cast_src = refs[pos:pos + n_cast]
    pos += n_cast
    out_refs = refs[pos:pos + len(outs)]
    pos += len(outs)
    cast_dst = refs[pos:pos + n_cast]
    pos += n_cast
    h_refs = {1: refs[pos]}
    for k, d in enumerate(dilations):
        h_refs[d] = refs[pos + 1 + k]
    stage_ref = refs[pos + 1 + len(dilations)] if dilations else None
    j = pl.program_id(1)
    tm = x_parts * x_refs[0].shape[0]

    @pl.when(j == 0)
    def _():
        _rms_rows(x_refs, g_ref, h_refs[1], stage_ref=stage_ref)
        for d in dilations:
            n_r = tm // d
            for r in range(d):
                for k in range(stage_ref.shape[0]):
                    h_refs[d][r * n_r:(r + 1) * n_r, k * LANES:(k + 1) * LANES] = (
                        stage_ref[k, pl.ds(r, n_r, stride=d), :].astype(BF16))

    def emit(out, o_ref):
        _run_casts(cast_src, cast_dst)
        acc = jnp.dot(h_refs[out.dilation][...], w_ref[...], preferred_element_type=F32)
        tn = acc.shape[1]
        if out.dilation > 1:
            n_r = tm // out.dilation
            for r in range(out.dilation):
                o_ref[0, r] = acc[r * n_r:(r + 1) * n_r, :].astype(o_ref.dtype)
        elif out.epilogue == "rope":
            scale = F32(out.segs[0].scale)
            for seg in out.segs[1:]:
                scale = jnp.where(j >= seg.j0, F32(seg.scale), scale)
            cos = cos_ref[...]
            sin = sin_ref[...]
            for c in range(0, tn, 2 * ROPE_HALF):
                t1 = acc[:, c:c + ROPE_HALF] * scale
                t2 = acc[:, c + ROPE_HALF:c + 2 * ROPE_HALF] * scale
                o_ref[:, c:c + ROPE_HALF] = (t1 * cos - t2 * sin).astype(o_ref.dtype)
                o_ref[:, c + ROPE_HALF:c + 2 * ROPE_HALF] = (t2 * cos + t1 * sin).astype(o_ref.dtype)
        elif out.epilogue == "silu":
            o_ref[...] = (acc * (1.0 / (1.0 + jnp.exp(-acc)))).astype(o_ref.dtype)
        else:
            o_ref[...] = acc.astype(o_ref.dtype)

    for out, o_ref in zip(outs, out_refs):
        hit = None
        for seg in out.segs:
            inside = (j >= seg.j0) & (j < seg.j0 + seg.nt)
            hit = inside if hit is None else hit | inside
        pl.when(hit)(functools.partial(emit, out, o_ref))


def _norm_matmul(x, gain, w, layer, outs, *, batch, seq, tm, tn, rope=None, single_buffer_x=False, casts=()):
    t, d = x.shape
    n = w.shape[2]
    tiles_per_batch = seq // tm
    dilations = tuple(sorted({o.dilation for o in outs} - {1}))
    ni, nj = t // tm, n // tn
    if single_buffer_x:
        x_parts = 1
        x_specs = [pl.BlockSpec((tm, d), lambda i, j: (i, 0), pipeline_mode=pl.Buffered(1))]
    else:
        x_parts = min(X_PARTS, nj - 1)
        x_specs = [
            pl.BlockSpec((tm // x_parts, d),
                         lambda i, j, q=q: (x_parts * jnp.minimum(i + (j >= nj - x_parts + q), ni - 1) + q, 0))
            for q in range(x_parts)]
    in_specs = x_specs + [
        pl.BlockSpec((1, d), lambda i, j: (0, 0)),
        pl.BlockSpec((None, d, tn), lambda i, j: (layer, 0, j)),
    ]
    args = [x] * x_parts + [gain.reshape(1, d), w]
    if rope is not None:
        in_specs += [pl.BlockSpec((tm, ROPE_HALF), lambda i, j: (i % tiles_per_batch, 0))] * 2
        args += list(rope)
    cast_in, cast_out, cast_shapes = _cast_specs(casts, t // tm, n // tn, lambda i, j: i, lambda i, j: j)
    in_specs += cast_in
    args += [w_ for w_, _ in casts]
    out_shapes, out_specs = [], []
    for out in outs:
        dd = out.dilation
        if dd == 1:
            out_shapes.append(jax.ShapeDtypeStruct((t, out.ncols), BF16))
            out_specs.append(pl.BlockSpec((tm, tn), lambda i, j, out=out: (i, _col_block(out, j))))
        else:
            out_shapes.append(jax.ShapeDtypeStruct((batch, dd, seq // dd, out.ncols), BF16))
            out_specs.append(pl.BlockSpec(
                (1, dd, tm // dd, tn),
                lambda i, j, out=out: (i // tiles_per_batch, 0, i % tiles_per_batch, _col_block(out, j))))
    scratch = [pltpu.VMEM((tm, d), BF16) for _ in range(1 + len(dilations))]
    if dilations:
        scratch.append(pltpu.VMEM((d // LANES, tm, LANES), F32))
    res = pl.pallas_call(
        functools.partial(_norm_matmul_kernel, outs=tuple(outs), has_rope=rope is not None,
                          dilations=dilations, n_cast=len(casts), x_parts=x_parts),
        out_shape=tuple(out_shapes + cast_shapes),
        grid=(t // tm, n // tn),
        in_specs=in_specs,
        out_specs=tuple(out_specs + cast_out),
        scratch_shapes=scratch,
        compiler_params=_params(("parallel", "arbitrary")),
        name="norm_matmul",
    )(*args)
    return res[:len(outs)], res[len(outs):]


def _retention_kernel(lg_ref, cd_ref, q_ref, k_ref, v_ref, g_ref, y_ref, r_ref, dm_ref, qd_ref, kd_ref, *, hps):
    hp = pl.program_id(1)
    n = pl.program_id(2)
    c = q_ref.shape[0]
    dk, dv = RET_QK_DIM, RET_V_DIM

    @pl.when(n == 0)
    def _():
        r_ref[...] = jnp.zeros_like(r_ref)
        i = lax.broadcasted_iota(jnp.int32, (c, c), 0)
        jj = lax.broadcasted_iota(jnp.int32, (c, c), 1)
        diff = i - jj
        row = lax.broadcasted_iota(jnp.int32, (c, LANES), 0).astype(F32)
        for hh in range(hps):
            lg = lg_ref[hp * hps + hh]
            dm_ref[hh] = jnp.where(diff >= 0, jnp.exp(lg * jnp.maximum(diff, 0).astype(F32)), 0.0)
            qd_ref[hh] = jnp.exp(lg * (row + 1.0))
            kd_ref[hh] = jnp.exp(lg * (F32(c - 1) - row))

    for hh in range(hps):
        q = q_ref[:, hh * dk:(hh + 1) * dk]
        k = k_ref[:, hh * dk:(hh + 1) * dk]
        v = v_ref[:, hh * dv:(hh + 1) * dv]
        s = lax.dot_general(q, k, (((1,), (1,)), ((), ())), preferred_element_type=F32) * dm_ref[hh]
        inner = jnp.dot(s.astype(BF16), v, preferred_element_type=F32)
        r_old = r_ref[hh]
        cross = jnp.dot(q, r_old.astype(BF16), preferred_element_type=F32)
        qd = qd_ref[hh]
        kd = kd_ref[hh]
        k_dec = jnp.concatenate(
            [(k[:, o:o + LANES].astype(F32) * kd).astype(BF16) for o in range(0, dk, LANES)], axis=1)
        r_ref[hh] = r_old * cd_ref[hp * hps + hh] + lax.dot_general(
            k_dec, v, (((0,), (0,)), ((), ())), preferred_element_type=F32)

        y = jnp.concatenate(
            [inner[:, o:o + LANES] + cross[:, o:o + LANES] * qd for o in range(0, dv, LANES)], axis=1)
        ms = jnp.mean(y * y, axis=-1, keepdims=True)
        gate = g_ref[:, hh * dv:(hh + 1) * dv].astype(F32)
        y_ref[:, hh * dv:(hh + 1) * dv] = ((y * lax.rsqrt(ms + NORM_EPS)) * gate).astype(y_ref.dtype)


def _retention(qk, v, gate, batch, seq, *, hps):
    t = qk.shape[0]
    c = RET_CHUNK
    nc = seq // c
    nhp = RET_HEADS // hps
    log_g = np.log(1.0 - 2.0 ** (-5.0 - np.arange(RET_HEADS))).astype(np.float32)
    c_dec = np.exp(log_g * c).astype(np.float32)
    smem = pl.BlockSpec(memory_space=pltpu.SMEM)
    qk_shape = (c, hps * RET_QK_DIM)
    v_shape = (c, hps * RET_V_DIM)
    return pl.pallas_call(
        functools.partial(_retention_kernel, hps=hps),
        out_shape=jax.ShapeDtypeStruct((t, RET_NV), BF16),
        grid=(batch, nhp, nc),
        in_specs=[
            smem, smem,
            pl.BlockSpec(qk_shape, lambda b, h, n: (b * nc + n, h)),
            pl.BlockSpec(qk_shape, lambda b, h, n: (b * nc + n, nhp + h)),
            pl.BlockSpec(v_shape, lambda b, h, n: (b * nc + n, h)),
            pl.BlockSpec(v_shape, lambda b, h, n: (b * nc + n, h)),
        ],
        out_specs=pl.BlockSpec(v_shape, lambda b, h, n: (b * nc + n, h)),
        scratch_shapes=[
            pltpu.VMEM((hps, RET_QK_DIM, RET_V_DIM), F32),
            pltpu.VMEM((hps, c, c), F32),
            pltpu.VMEM((hps, c, LANES), F32),
            pltpu.VMEM((hps, c, LANES), F32),
        ],
        compiler_params=_params(("parallel", "parallel", "arbitrary")),
        name="retention",
    )(jnp.asarray(log_g), jnp.asarray(c_dec), qk, qk, v, gate)


def _matmul_residual_kernel(*refs, n_cast):
    y_ref, w_ref, x_ref = refs[:3]
    cast_src = refs[3:3 + n_cast]
    o_ref = refs[3 + n_cast]
    cast_dst = refs[4 + n_cast:]
    o_ref[...] = x_ref[...] + jnp.dot(y_ref[...], w_ref[...], preferred_element_type=F32)
    _run_casts(cast_src, cast_dst)


def _matmul_residual(y, w, layer, x, *, tm, tn, casts=()):
    t, k = y.shape
    n = w.shape[2]
    cast_in, cast_out, cast_shapes = _cast_specs(casts, t // tm, n // tn, lambda j, i: i, lambda j, i: j)
    res = pl.pallas_call(
        functools.partial(_matmul_residual_kernel, n_cast=len(casts)),
        out_shape=(jax.ShapeDtypeStruct((t, n), F32), *cast_shapes),
        grid=(n // tn, t // tm),
        in_specs=[
            pl.BlockSpec((tm, k), lambda j, i: (i, 0)),
            pl.BlockSpec((None, k, tn), lambda j, i: (layer, 0, j)),
            pl.BlockSpec((tm, tn), lambda j, i: (i, j)),
            *cast_in,
        ],
        out_specs=(pl.BlockSpec((tm, tn), lambda j, i: (i, j)), *cast_out),
        compiler_params=_params(("parallel", "parallel")),
        name="matmul_residual",
    )(y, w, x, *[w_ for w_, _ in casts])
    return res[0], res[1:]


def _ffn_kernel(*refs, final_norm):
    if final_norm:
        x_ref, g_ref, w1_ref, w2_ref, wo_ref, gf_ref, o_ref, h_ref = refs
    else:
        x_ref, g_ref, w1_ref, w2_ref, wo_ref, o_ref, h_ref = refs
    f = pl.program_id(1)

    @pl.when(f == 0)
    def _():
        _rms_rows([x_ref], g_ref, h_ref, copy_ref=o_ref)

    h = h_ref[...]
    z1 = jnp.dot(h, w1_ref[...], preferred_element_type=F32)
    z2 = jnp.dot(h, w2_ref[...], preferred_element_type=F32)
    a = (z1 * (1.0 / (1.0 + jnp.exp(-z1)))) * z2
    o_ref[...] += jnp.dot(a.astype(BF16), wo_ref[...], preferred_element_type=F32)

    if final_norm:
        @pl.when(f == pl.num_programs(1) - 1)
        def _():
            gf = gf_ref[...]

            def body(c, carry):
                r0 = pl.multiple_of(c * RMS_ROWS, RMS_ROWS)
                x = o_ref[pl.ds(r0, RMS_ROWS), :]
                ms = jnp.mean(x * x, axis=-1, keepdims=True)
                o_ref[pl.ds(r0, RMS_ROWS), :] = x * lax.rsqrt(ms + NORM_EPS) * gf
                return carry

            lax.fori_loop(0, o_ref.shape[0] // RMS_ROWS, body, 0)


def _ffn(x, gain, w_in, w_out, layer, *, tm, tf, final_gain=None):
    t, d = x.shape
    nf = FFN_DIM // tf
    in_specs = [
        pl.BlockSpec((tm, d), lambda i, f: (i, 0)),
        pl.BlockSpec((1, d), lambda i, f: (0, 0)),
        pl.BlockSpec((None, d, tf), lambda i, f: (layer, 0, f)),
        pl.BlockSpec((None, d, tf), lambda i, f: (layer, 0, nf + f)),
        pl.BlockSpec((None, tf, d), lambda i, f: (layer, f, 0)),
    ]
    args = [x, gain.reshape(1, d), w_in, w_in, w_out]
    if final_gain is not None:
        in_specs.append(pl.BlockSpec((1, d), lambda i, f: (0, 0)))
        args.append(final_gain.reshape(1, d))
    return pl.pallas_call(
        functools.partial(_ffn_kernel, final_norm=final_gain is not None),
        out_shape=jax.ShapeDtypeStruct((t, d), F32),
        grid=(t // tm, nf),
        in_specs=in_specs,
        out_specs=pl.BlockSpec((tm, d), lambda i, f: (i, 0)),
        scratch_shapes=[pltpu.VMEM((tm, d), BF16)],
        compiler_params=_params(("parallel", "arbitrary")),
        name="ffn",
    )(*args)


def _t5_bucket_table(dist):
    n = np.maximum(dist, 0)
    max_exact = REL_BUCKETS // 2
    large = max_exact + (np.log(np.maximum(n, 1) / max_exact) / np.log(REL_MAX_DIST / max_exact)
                         * (REL_BUCKETS - max_exact)).astype(np.int32)
    large = np.minimum(large, REL_BUCKETS - 1)
    return np.where(n < max_exact, n, large).astype(np.int32)


def _bias_kernel(tab_ref, bucket_ref, o_ref):
    col = pl.program_id(0) * ATT_HEADS + pl.program_id(1)
    bucket = bucket_ref[0]
    acc = jnp.zeros(bucket.shape, F32)
    for b in range(REL_BUCKETS):
        acc = jnp.where(bucket == b, tab_ref[b, col], acc)
    o_ref[0, 0] = acc * float(np.log2(np.e))


def _rel_bias_tiles(rel_bias):
    blk = ATT_BLK
    i = np.arange(blk)[:, None]
    c = np.arange(2 * blk)[None, :]
    delta = blk + i - c
    buckets = np.stack([_t5_bucket_table(np.maximum(delta, 0) * d) for _, d in DIL_CONFIGS])
    return pl.pallas_call(
        _bias_kernel,
        out_shape=jax.ShapeDtypeStruct((N_GROUPS, ATT_HEADS, blk, 2 * blk), F32),
        grid=(N_GROUPS, ATT_HEADS),
        in_specs=[
            pl.BlockSpec(memory_space=pltpu.SMEM),
            pl.BlockSpec((1, blk, 2 * blk), lambda g, h: (g, 0, 0)),
        ],
        out_specs=pl.BlockSpec((1, 1, blk, 2 * blk), lambda g, h: (g, h, 0, 0)),
        compiler_params=_params(("parallel", "parallel")),
        name="rel_bias",
    )(rel_bias, jnp.asarray(buckets))


def _att_kernel(q_ref, kp_ref, kc_ref, vp_ref, vc_ref, bias_ref, cap_ref, o_ref, lse_ref, *, d, hpb):
    n = pl.program_id(1)
    hb = pl.program_id(2)
    blk = ATT_BLK
    lane = lax.broadcasted_iota(jnp.int32, (blk, LANES), 1)
    log2e = float(np.log2(np.e))
    scale = ATT_HEAD_DIM ** -0.5 * log2e
    first = jnp.where(n > 0, 0, 1)

    def residue(r):
        rows = pl.ds(r, blk, stride=d) if d > 1 else slice(None)
        lse_tile = jnp.zeros((blk, LANES), F32)
        pair = []
        for hl in range(hpb):
            hs = slice(hl * ATT_HEAD_DIM, (hl + 1) * ATT_HEAD_DIM)
            q = q_ref[0, r, :, hs]
            kb = jnp.concatenate([kp_ref[0, r, :, hs], kc_ref[0, r, :, hs]], axis=0)
            vb = jnp.concatenate([vp_ref[0, r, :, hs], vc_ref[0, r, :, hs]], axis=0)
            s = lax.dot_general(q, kb, (((1,), (1,)), ((), ())), preferred_element_type=F32)
            s = jnp.minimum(s * scale + bias_ref[0, hl], cap_ref[first])
            m = jnp.max(s, axis=-1, keepdims=True)
            p = jnp.exp2(s - m)
            den = jnp.sum(p, axis=-1, keepdims=True)
            pair.append(jnp.dot(p.astype(BF16), vb, preferred_element_type=F32) / den)
            if len(pair) == 2:
                o_ref[hl // 2, rows, :] = pltpu.pack_elementwise(pair, packed_dtype=BF16)
                pair = []
            lse_tile = jnp.where(lane == hb * hpb + hl, (m + jnp.log2(den)) * (1.0 / log2e), lse_tile)
        lse_ref[0, rows, :] = lse_tile

    rpi = min(d, max(1, 2 * ATT_HEADS // hpb))

    def body(it, carry):
        for rr in range(rpi):
            residue(it * rpi + rr)
        return carry

    if d == rpi:
        body(0, 0)
    else:
        lax.fori_loop(0, d // rpi, body, 0)


def _band_caps():
    blk = ATT_BLK
    i = np.arange(blk)[:, None]
    c = np.arange(2 * blk)[None, :]
    delta = blk + i - c
    band = (delta >= 0) & (delta <= blk)
    caps = np.stack([band, band & (c >= blk)])
    return np.where(caps, np.finfo(np.float32).max, NEG_INF).astype(np.float32)


def _dilated_group(q, kv, bias, gi, batch, seq):
    d = DIL_CONFIGS[gi][1]
    blk = ATT_BLK
    nb = seq // d // blk
    hpb = min(ATT_HEADS, ATT_TILES_PER_STEP // d)
    nhb = ATT_HEADS // hpb
    band = (1, d, blk, hpb * ATT_HEAD_DIM)
    prev = lambda n: jnp.maximum(n - 1, 0)
    return pl.pallas_call(
        functools.partial(_att_kernel, d=d, hpb=hpb),
        out_shape=(jax.ShapeDtypeStruct((ATT_HEADS // 2, batch * seq, ATT_HEAD_DIM), jnp.int32),
                   jax.ShapeDtypeStruct((nhb, batch * seq, LANES), F32)),
        grid=(batch, nb, nhb),
        in_specs=[
            pl.BlockSpec(band, lambda b, n, hb: (b, 0, n, hb)),
            pl.BlockSpec(band, lambda b, n, hb: (b, 0, prev(n), hb)),
            pl.BlockSpec(band, lambda b, n, hb: (b, 0, n, hb)),
            pl.BlockSpec(band, lambda b, n, hb: (b, 0, prev(n), nhb + hb)),
            pl.BlockSpec(band, lambda b, n, hb: (b, 0, n, nhb + hb)),
            pl.BlockSpec((1, hpb, blk, 2 * blk), lambda b, n, hb: (gi, hb, 0, 0)),
            pl.BlockSpec((2, blk, 2 * blk), lambda b, n, hb: (0, 0, 0)),
        ],
        out_specs=(pl.BlockSpec((hpb // 2, d * blk, ATT_HEAD_DIM), lambda b, n, hb: (hb, b * nb + n, 0)),
                   pl.BlockSpec((1, d * blk, LANES), lambda b, n, hb: (hb, b * nb + n, 0))),
        compiler_params=_params(("parallel", "parallel", "parallel")),
        name="dilated_attention_g%d" % gi,
    )(q, kv, kv, kv, kv, bias, jnp.asarray(_band_caps()))


def _merge_out_kernel(o0_ref, o1_ref, o2_ref, l0_ref, l1_ref, l2_ref, x_ref, w_ref, out_ref, ma_ref, mb_ref):
    s = pl.program_id(0)

    def step(src_ref, dst_ref):
        out_ref[...] = x_ref[...] + jnp.dot(src_ref[...], w_ref[...], preferred_element_type=F32)
        l0 = jnp.sum(l0_ref[...], axis=0)
        l1 = jnp.sum(l1_ref[...], axis=0)
        l2 = jnp.sum(l2_ref[...], axis=0)
        mx = jnp.maximum(jnp.maximum(l0, l1), l2)
        e0 = jnp.exp(l0 - mx)
        e1 = jnp.exp(l1 - mx)
        e2 = jnp.exp(l2 - mx)
        den = e0 + e1 + e2
        w0 = e0 / den
        w1 = e1 / den
        w2 = e2 / den
        tm = l0.shape[0]
        for h in range(ATT_HEADS):
            hs = slice(h * ATT_HEAD_DIM, (h + 1) * ATT_HEAD_DIM)
            b0 = jnp.broadcast_to(w0[:, h:h + 1], (tm, ATT_HEAD_DIM))
            b1 = jnp.broadcast_to(w1[:, h:h + 1], (tm, ATT_HEAD_DIM))
            b2 = jnp.broadcast_to(w2[:, h:h + 1], (tm, ATT_HEAD_DIM))
            o0, o1, o2 = (
                pltpu.unpack_elementwise(o_ref[h // 2], index=h % 2, packed_dtype=BF16, unpacked_dtype=F32)
                for o_ref in (o0_ref, o1_ref, o2_ref))
            dst_ref[:, hs] = (b0 * o0 + b1 * o1 + b2 * o2).astype(dst_ref.dtype)

    @pl.when(s == 0)
    def _():
        mb_ref[...] = jnp.zeros_like(mb_ref)

    @pl.when(s % 2 == 0)
    def _():
        step(mb_ref, ma_ref)

    @pl.when(s % 2 == 1)
    def _():
        step(ma_ref, mb_ref)


def _merge_out(outs, lses, x, w, layer, *, tm):
    t, d = x.shape
    nt = t // tm
    cur = lambda s: jnp.minimum(s, nt - 1)
    prv = lambda s: jnp.maximum(s - 1, 0)
    heads = pl.BlockSpec((ATT_HEADS // 2, tm, ATT_HEAD_DIM), lambda s: (0, cur(s), 0))
    narrow = [pl.BlockSpec((l.shape[0], tm, LANES), lambda s: (0, cur(s), 0)) for l in lses]
    return pl.pallas_call(
        _merge_out_kernel,
        out_shape=jax.ShapeDtypeStruct((t, d), F32),
        grid=(nt + 1,),
        in_specs=[heads, heads, heads, *narrow,
                  pl.BlockSpec((tm, d), lambda s: (prv(s), 0)),
                  pl.BlockSpec((None, ATT_WIDTH, d), lambda s: (layer, 0, 0))],
        out_specs=pl.BlockSpec((tm, d), lambda s: (prv(s), 0)),
        scratch_shapes=[pltpu.VMEM((tm, ATT_WIDTH), BF16), pltpu.VMEM((tm, ATT_WIDTH), BF16)],
        compiler_params=_params(("arbitrary",)),
        name="merge_out",
    )(*outs, *lses, x, w)


def _as_residue_major(a, batch, seq):
    return a.reshape(batch, 1, seq, a.shape[1]) if a.ndim == 2 else a


def kernel(x, g_mix, g_ffn, w_ret_in, w_ret_out, g_kv, w_kv, w_att_q, w_att_out, rel_bias, w_ffn_in, w_ffn_out, g_final):
    batch, seq, d = x.shape
    t = batch * seq
    x = x.reshape(t, d)
    tn = 1024
    ret_dims = dict(batch=batch, seq=seq, tm=1024, tn=tn)
    kv_dims = dict(batch=batch, seq=seq, tm=1024, tn=tn, single_buffer_x=True)
    q_dims = dict(batch=batch, seq=seq, tm=512, tn=tn)

    inv = (1.0 / ROPE_BASE ** np.linspace(0.0, 1.0, ROPE_HALF)).astype(np.float32)
    ang = jnp.arange(seq, dtype=F32)[:, None] * jnp.asarray(inv)[None, :]
    rope = (jnp.cos(ang), jnp.sin(ang))

    qt, vt, wt = RET_NQ // tn, RET_NV // tn, ATT_WIDTH // tn
    ret_outs = (
        _Out(2 * RET_NQ, 1, "rope", (_Seg(0, qt, 0, 1.0), _Seg(qt, qt, qt, RET_QK_DIM ** -0.5))),
        _Out(RET_NV, 1, "cast", (_Seg(2 * qt, vt, 0, 1.0),)),
        _Out(RET_NV, 1, "silu", (_Seg(2 * qt + vt, vt, 0, 1.0),)),
    )
    kv_outs = tuple(
        _Out(2 * ATT_WIDTH, dd, "cast", (_Seg(gi * wt, wt, 0, 1.0), _Seg((N_GROUPS + gi) * wt, wt, wt, 1.0)))
        for gi, (_, dd) in enumerate(DIL_CONFIGS))
    q_outs = tuple(
        _Out(ATT_WIDTH, dd, "cast", (_Seg(gi * wt, wt, 0, 1.0),)) for gi, (_, dd) in enumerate(DIL_CONFIGS))

    def whole(w):
        return w.reshape((1, -1) + w.shape[-1:])

    ret_in = [w_ret_in[0].astype(BF16)[None], None]
    ret_out = [w_ret_out[0].astype(BF16)[None], None]
    ffn_in = [None] * DEPTH
    ffn_out = [None] * DEPTH
    w_kv_b = att_q = att_out = None

    kvs = bias = None
    for l in range(DEPTH):
        if l == 0:
            (qk, v, gate), (ret_in[1], ret_out[1], ffn_in[0], ffn_out[0]) = _norm_matmul(
                x, g_mix[l], ret_in[l], 0, ret_outs, rope=rope, **ret_dims,
                casts=[(w_ret_in, 1), (w_ret_out, 1), (w_ffn_in, 0), (w_ffn_out, 0)])
            y = _retention(qk, v, gate, batch, seq, hps=8)
            x, (ffn_in[2],) = _matmul_residual(y, ret_out[l], 0, x, tm=512, tn=1024, casts=[(w_ffn_in, 2)])
        elif l == 1:
            (qk, v, gate), (ffn_in[1], ffn_out[1], w_kv_b, att_q) = _norm_matmul(
                x, g_mix[l], ret_in[l], 0, ret_outs, rope=rope, **ret_dims,
                casts=[(w_ffn_in, 1), (w_ffn_out, 1), (whole(w_kv), 0), (whole(w_att_q), 0)])
            att_q = att_q.reshape(w_att_q.shape)
            y = _retention(qk, v, gate, batch, seq, hps=8)
            x, (ffn_out[2], att_out) = _matmul_residual(
                y, ret_out[l], 0, x, tm=512, tn=1024, casts=[(w_ffn_out, 2), (whole(w_att_out), 0)])
            att_out = att_out.reshape(w_att_out.shape)
        else:
            j = l - N_RET_LAYERS
            if j == 0:
                kvs, (ffn_in[3], ffn_out[3]) = _norm_matmul(
                    x, g_kv, w_kv_b, 0, kv_outs, **kv_dims, casts=[(w_ffn_in, 3), (w_ffn_out, 3)])
                bias = _rel_bias_tiles(rel_bias)
            qs, _ = _norm_matmul(x, g_mix[l], att_q, j, q_outs, **q_dims)
            outs, lses = [], []
            for gi in range(N_GROUPS):
                o, lse = _dilated_group(_as_residue_major(qs[gi], batch, seq),
                                        _as_residue_major(kvs[gi], batch, seq), bias, gi, batch, seq)
                outs.append(o)
                lses.append(lse)
            x = _merge_out(outs, lses, x, att_out, j, tm=512)
        x = _ffn(x, g_ffn[l], ffn_in[l], ffn_out[l], 0, tm=1024, tf=512,
                 final_gain=g_final if l == DEPTH - 1 else None)
    return x.reshape(batch, seq, d)
```
